```python
import math
import jax, jax.numpy as jnp
from jax import lax
import numpy as np

D_MODEL = 1024
BATCH = 2
SEQ = 8192
DEPTH = 2
DEC_BATCH = 32
DEC_SEQ = 1
PAST_LEN = 16384
PAGE_SIZE = 128

N_A_LAYERS = DEPTH // 2
N_B_LAYERS = DEPTH - N_A_LAYERS
FFN_DIM = 2816
GROUP_SIZE = 16
N_GROUPS = D_MODEL // GROUP_SIZE
STATE_DIM = 64
DT_MIN = 1e-3
DT_MAX = 1e-1
N_HEADS = 8
HEAD_DIM = D_MODEL // N_HEADS
BLOCK = 256
TOP_K = 3
Q_CHUNK = 32
RMS_EPS = 1e-6
SCALE = HEAD_DIM ** -0.5

kernel_name = 'yoco_s5_moba_macaron_step'


def rmsnorm(x, g):
    xf = x.astype(jnp.float32)
    inv = lax.rsqrt(jnp.mean(xf * xf, axis=-1, keepdims=True) + RMS_EPS)
    return (xf * inv * g.astype(jnp.float32)).astype(x.dtype)


def half_ffn(x, g, w_gate, w_up, w_down):
    h = rmsnorm(x, g)
    return x + 0.5 * ((jax.nn.silu(h @ w_gate) * (h @ w_up)) @ w_down)


def _ssm_combine(e1, e2):
    ar1, ai1, br1, bi1 = e1
    ar2, ai2, br2, bi2 = e2
    ar = ar2 * ar1 - ai2 * ai1
    ai = ar2 * ai1 + ai2 * ar1
    xr, xi = ar2[:, None], ai2[:, None]
    br = xr * br1 - xi * bi1 + br2
    bi = xr * bi1 + xi * br1 + bi2
    return ar, ai, br, bi


def s5_mixer(u, h0_re, h0_im, a_re, a_im, log_step, b_re, b_im, c_re, c_im, d_skip, w_glu_a, w_glu_b):
    f32 = jnp.float32
    bsz, seq, _ = u.shape
    uf = u.astype(f32).reshape(bsz, seq, N_GROUPS, GROUP_SIZE)
    lam_re = jnp.minimum(a_re.astype(f32), -1e-4)
    lam_im = a_im.astype(f32)
    dt = jnp.exp(log_step.astype(f32))[:, None]
    mag = jnp.exp(lam_re * dt)
    lb_re = mag * jnp.cos(lam_im * dt)
    lb_im = mag * jnp.sin(lam_im * dt)
    den = lam_re * lam_re + lam_im * lam_im
    nr = lb_re - 1.0
    coef_re = (nr * lam_re + lb_im * lam_im) / den
    coef_im = (lb_im * lam_re - nr * lam_im) / den
    br, bi = b_re.astype(f32), b_im.astype(f32)
    bb_re = coef_re[..., None] * br - coef_im[..., None] * bi
    bb_im = coef_re[..., None] * bi + coef_im[..., None] * br
    bu_re = jnp.einsum('blgi,gpi->lbgp', uf, bb_re)
    bu_im = jnp.einsum('blgi,gpi->lbgp', uf, bb_im)
    h0r, h0i = h0_re.astype(f32), h0_im.astype(f32)
    bu_re = bu_re.at[0].add(lb_re * h0r - lb_im * h0i)
    bu_im = bu_im.at[0].add(lb_re * h0i + lb_im * h0r)
    a_t_re = jnp.broadcast_to(lb_re, (seq,) + lb_re.shape)
    a_t_im = jnp.broadcast_to(lb_im, (seq,) + lb_im.shape)
    _, _, h_re, h_im = lax.associative_scan(_ssm_combine, (a_t_re, a_t_im, bu_re, bu_im), axis=0)
    y = (jnp.einsum('lbgp,gip->blgi', h_re, c_re.astype(f32))
         - jnp.einsum('lbgp,gip->blgi', h_im, c_im.astype(f32)))
    y = (y + d_skip.astype(f32).reshape(N_GROUPS, GROUP_SIZE) * uf).reshape(bsz, seq, D_MODEL)
    g = jax.nn.gelu(y).astype(u.dtype)
    out = (g @ w_glu_a) * jax.nn.sigmoid(g @ w_glu_b)
    return out, h_re[-1], h_im[-1]


def select_blocks(q, means, pos, n_blocks):
    k_sel = min(TOP_K, n_blocks)
    own = pos // BLOCK
    gate = jnp.einsum('bqhd,bnhd->bqhn', q, means, preferred_element_type=jnp.float32)
    past = jnp.arange(n_blocks)[None, :] < own[:, None]
    gate = jnp.where(past[None, :, None, :], gate, -jnp.inf)
    _, sel = lax.top_k(gate, k_sel)
    own_b = jnp.broadcast_to(own[None, :, None, None], sel.shape[:3] + (1,)).astype(sel.dtype)
    blocks = jnp.concatenate([sel, own_b], axis=-1)
    valid = jnp.concatenate([jnp.arange(k_sel)[None, :] < own[:, None],
                             jnp.ones((pos.shape[0], 1), dtype=bool)], axis=-1)
    rows = blocks[..., None] * BLOCK + jnp.arange(BLOCK, dtype=blocks.dtype)
    mask = valid[None, :, None, :, None] & (rows <= pos[None, :, None, None, None])
    return rows, mask


def attend_gathered(q, k_g, v_g, mask):
    s = jnp.einsum('bqhd,bqhnrd->bqhnr', q, k_g, preferred_element_type=jnp.float32) * SCALE
    s = jnp.where(mask, s, -jnp.inf)
    shp = s.shape
    p = jax.nn.softmax(s.reshape(shp[:3] + (-1,)), axis=-1).reshape(shp)
    return jnp.einsum('bqhnr,bqhnrd->bqhd', p.astype(v_g.dtype), v_g)


def moba_prompt(q, k, v):
    bsz, seq, nh, hd = q.shape
    nb = -(-seq // BLOCK)
    pad = ((0, 0), (0, nb * BLOCK - seq), (0, 0), (0, 0))
    kp, vp = jnp.pad(k, pad), jnp.pad(v, pad)
    means = (kp.reshape(bsz, nb, BLOCK, nh, hd).sum(axis=2, dtype=jnp.float32) / BLOCK).astype(q.dtype)
    bi = jnp.arange(bsz)[:, None, None, None, None]
    hi = jnp.arange(nh)[None, None, :, None, None]

    def chunk(c):
        s0 = c * Q_CHUNK
        qc = lax.dynamic_slice_in_dim(q, s0, Q_CHUNK, axis=1)
        pos = s0 + jnp.arange(Q_CHUNK, dtype=jnp.int32)
        rows, mask = select_blocks(qc, means, pos, nb)
        return attend_gathered(qc, kp[bi, rows, hi], vp[bi, rows, hi], mask)

    out = lax.map(chunk, jnp.arange(seq // Q_CHUNK, dtype=jnp.int32))
    return out.transpose(1, 0, 2, 3, 4).reshape(bsz, seq, nh * hd)


def gather_paged_rows(cache, new, page_table, rows):
    past_len = page_table.shape[1] * PAGE_SIZE
    bi = jnp.arange(rows.shape[0])[:, None, None, None, None]
    hi = jnp.arange(N_HEADS)[None, None, :, None, None]
    rc = jnp.clip(rows, 0, past_len - 1)
    phys = page_table[bi, rc // PAGE_SIZE]
    from_cache = cache[phys, rc % PAGE_SIZE, hi]
    rn = jnp.clip(rows - past_len, 0, new.shape[1] - 1)
    from_new = new[bi, rn, hi]
    return jnp.where((rows < past_len)[..., None], from_cache, from_new)


def moba_sample(q, k_new, v_new, cache_k, cache_v, page_table):
    bsz, s_new, nh, hd = q.shape
    n_pages = page_table.shape[1]
    past_len = n_pages * PAGE_SIZE
    nb = -(-(past_len + s_new) // BLOCK)
    ppb = BLOCK // PAGE_SIZE
    page_sums = cache_k.sum(axis=1, dtype=jnp.float32)
    past = page_sums[page_table]
    past = jnp.pad(past, ((0, 0), (0, nb * ppb - n_pages), (0, 0), (0, 0)))
    past = past.reshape(bsz, nb, ppb, nh, hd).sum(axis=2)
    pos = past_len + jnp.arange(s_new, dtype=jnp.int32)
    onehot = (pos[:, None] // BLOCK == jnp.arange(nb)[None, :]).astype(jnp.float32)
    means = (past + jnp.einsum('sn,bshd->bnhd', onehot, k_new.astype(jnp.float32))) / BLOCK
    rows, mask = select_blocks(q, means.astype(q.dtype), pos, nb)
    k_g = gather_paged_rows(cache_k, k_new, page_table, rows)
    v_g = gather_paged_rows(cache_v, v_new, page_table, rows)
    return attend_gathered(q, k_g, v_g, mask).reshape(bsz, s_new, nh * hd)


def setup_inputs(seed: int = 0) -> dict:
    key = jax.random.key(seed)
    keys = iter(jax.random.split(key, 40))
    f32 = jnp.float32
    n_pages = PAST_LEN // PAGE_SIZE
    n_used = DEC_BATCH * n_pages
    n_pool = (n_used * 5) // 4

    def nrm(shape, scale):
        return jax.random.normal(next(keys), shape, f32) * scale

    def gain(shape):
        return 1.0 + nrm(shape, 0.02)

    ds = D_MODEL ** -0.5
    fs = FFN_DIM ** -0.5
    inp = {}
    inp['x_prompt'] = nrm((BATCH, SEQ, D_MODEL), 1.0)
    inp['x_sample'] = nrm((DEC_BATCH, DEC_SEQ, D_MODEL), 1.0)
    inp['state_ssm_re'] = nrm((N_A_LAYERS, DEC_BATCH, N_GROUPS, STATE_DIM), 0.1)
    inp['state_ssm_im'] = nrm((N_A_LAYERS, DEC_BATCH, N_GROUPS, STATE_DIM), 0.1)
    inp['cache_k'] = nrm((n_pool, PAGE_SIZE, N_HEADS, HEAD_DIM), 1.0)
    inp['cache_v'] = nrm((n_pool, PAGE_SIZE, N_HEADS, HEAD_DIM), 1.0)
    inp['page_table'] = jax.random.permutation(next(keys), n_pool)[:n_used].reshape(DEC_BATCH, n_pages).astype(jnp.int32)
    inp['norm_ffn1'] = gain((DEPTH, D_MODEL))
    inp['w_ffn1_gate'] = nrm((DEPTH, D_MODEL, FFN_DIM), ds)
    inp['w_ffn1_up'] = nrm((DEPTH, D_MODEL, FFN_DIM), ds)
    inp['w_ffn1_down'] = nrm((DEPTH, FFN_DIM, D_MODEL), fs)
    inp['norm_mix'] = gain((DEPTH, D_MODEL))
    inp['norm_ffn2'] = gain((DEPTH, D_MODEL))
    inp['w_ffn2_gate'] = nrm((DEPTH, D_MODEL, FFN_DIM), ds)
    inp['w_ffn2_up'] = nrm((DEPTH, D_MODEL, FFN_DIM), ds)
    inp['w_ffn2_down'] = nrm((DEPTH, FFN_DIM, D_MODEL), fs)
    inp['ssm_a_re'] = -0.5 + nrm((N_A_LAYERS, N_GROUPS, STATE_DIM), 0.01)
    inp['ssm_a_im'] = jnp.pi * jnp.arange(STATE_DIM, dtype=f32) + nrm((N_A_LAYERS, N_GROUPS, STATE_DIM), 0.01)
    inp['ssm_log_step'] = jax.random.uniform(next(keys), (N_A_LAYERS, N_GROUPS), f32,
                                             math.log(DT_MIN), math.log(DT_MAX))
    inp['ssm_b_re'] = nrm((N_A_LAYERS, N_GROUPS, STATE_DIM, GROUP_SIZE), (2 * GROUP_SIZE) ** -0.5)
    inp['ssm_b_im'] = nrm((N_A_LAYERS, N_GROUPS, STATE_DIM, GROUP_SIZE), (2 * GROUP_SIZE) ** -0.5)
    inp['ssm_c_re'] = nrm((N_A_LAYERS, N_GROUPS, GROUP_SIZE, STATE_DIM), STATE_DIM ** -0.5)
    inp['ssm_c_im'] = nrm((N_A_LAYERS, N_GROUPS, GROUP_SIZE, STATE_DIM), STATE_DIM ** -0.5)
    inp['ssm_d'] = nrm((N_A_LAYERS, D_MODEL), 1.0)
    inp['ssm_w_glu_a'] = nrm((N_A_LAYERS, D_MODEL, D_MODEL), ds)
    inp['ssm_w_glu_b'] = nrm((N_A_LAYERS, D_MODEL, D_MODEL), ds)
    inp['norm_kv'] = gain((D_MODEL,))
    inp['w_k'] = nrm((D_MODEL, D_MODEL), ds)
    inp['w_v'] = nrm((D_MODEL, D_MODEL), ds)
    inp['attn_w_q'] = nrm((N_B_LAYERS, D_MODEL, D_MODEL), ds)
    inp['attn_w_o'] = nrm((N_B_LAYERS, D_MODEL, D_MODEL), ds)
    inp['norm_final'] = gain((D_MODEL,))
    return inp


def reference(x_prompt, x_sample, state_ssm_re, state_ssm_im, cache_k, cache_v, page_table,
              norm_ffn1, w_ffn1_gate, w_ffn1_up, w_ffn1_down, norm_mix,
              norm_ffn2, w_ffn2_gate, w_ffn2_up, w_ffn2_down,
              ssm_a_re, ssm_a_im, ssm_log_step, ssm_b_re, ssm_b_im, ssm_c_re, ssm_c_im,
              ssm_d, ssm_w_glu_a, ssm_w_glu_b,
              norm_kv, w_k, w_v, attn_w_q, attn_w_o, norm_final):

    def trunk(x, h0_re, h0_im, attend):
        new_re, new_im = [], []
        k = v = None
        for l in range(DEPTH):
            x = half_ffn(x, norm_ffn1[l], w_ffn1_gate[l], w_ffn1_up[l], w_ffn1_down[l])
            u = rmsnorm(x, norm_mix[l])
            if l < N_A_LAYERS:
                y, hr, hi = s5_mixer(u, h0_re[l], h0_im[l], ssm_a_re[l], ssm_a_im[l], ssm_log_step[l],
                                     ssm_b_re[l], ssm_b_im[l], ssm_c_re[l], ssm_c_im[l], ssm_d[l],
                                     ssm_w_glu_a[l], ssm_w_glu_b[l])
                new_re.append(hr)
                new_im.append(hi)
            else:
                j = l - N_A_LAYERS
                bsz, seq, _ = u.shape
                q = (u @ attn_w_q[j]).reshape(bsz, seq, N_HEADS, HEAD_DIM)
                y = attend(q, k, v) @ attn_w_o[j]
            x = x + y
            x = half_ffn(x, norm_ffn2[l], w_ffn2_gate[l], w_ffn2_up[l], w_ffn2_down[l])
            if l == N_A_LAYERS - 1:
                kv = rmsnorm(x, norm_kv)
                bsz, seq, _ = kv.shape
                k = (kv @ w_k).reshape(bsz, seq, N_HEADS, HEAD_DIM)
                v = (kv @ w_v).reshape(bsz, seq, N_HEADS, HEAD_DIM)
        return rmsnorm(x, norm_final), jnp.stack(new_re), jnp.stack(new_im), k, v

    h0_prompt = jnp.zeros((N_A_LAYERS, x_prompt.shape[0], N_GROUPS, STATE_DIM), jnp.float32)
    y_prompt, ssm_re_p, ssm_im_p, k_p, v_p = trunk(x_prompt, h0_prompt, h0_prompt, moba_prompt)

    def attend_sample(q, k, v):
        return moba_sample(q, k, v, cache_k, cache_v, page_table)

    y_sample, ssm_re_s, ssm_im_s, k_s, v_s = trunk(x_sample, state_ssm_re, state_ssm_im, attend_sample)
    return (y_prompt, y_sample, ssm_re_p, ssm_im_p, ssm_re_s, ssm_im_s, k_p, v_p, k_s, v_s)
```

```python
import functools

import jax
import jax.numpy as jnp
from jax import lax
from jax.experimental import pallas as pl
from jax.experimental.pallas import tpu as pltpu

F32 = jnp.float32
BF16 = jnp.bfloat16

D_MODEL = 1024
N_GROUPS = 64
GROUP_SIZE = 16
STATE_DIM = 64
N_HEADS = 8
HEAD_DIM = 128
MOBA_BLOCK = 256
TOP_K = 3
PAGE_SIZE = 128
RMS_EPS = 1e-6
ATTN_SCALE = HEAD_DIM ** -0.5
MASK_VALUE = -1e30

SSM_CHUNK = 16
CHUNK_W = SSM_CHUNK * GROUP_SIZE
LANES = 128
ROW_TILE = 512
SSM_GROUP_TILE = 4
SCAN_TILE = 128
PAGES_PER_STEP = 16
VMEM_LIMIT = 56 * 1024 * 1024

_NT = (((1,), (1,)), ((), ()))


def _params(*sem):
    return pltpu.CompilerParams(dimension_semantics=sem, vmem_limit_bytes=VMEM_LIMIT)


def _resident(shape):
    zeros = (0,) * len(shape)
    return pl.BlockSpec(shape, lambda *_: zeros, pipeline_mode=pl.Buffered(1))


def _rms(x, g):
    inv = lax.rsqrt(jnp.mean(x * x, axis=-1, keepdims=True) + RMS_EPS)
    return x * inv * g


def _row_tile(m):
    return ROW_TILE if m % ROW_TILE == 0 else m


def _ffn_kernel(x_ref, g_ref, wg_ref, wu_ref, wd_ref, o_ref):
    x = x_ref[...]
    h = _rms(x, g_ref[...]).astype(BF16)
    a = jnp.dot(h, wg_ref[...], preferred_element_type=F32)
    b = jnp.dot(h, wu_ref[...], preferred_element_type=F32)
    act = (jax.nn.silu(a) * b).astype(BF16)
    o_ref[...] = x + 0.5 * jnp.dot(act, wd_ref[...], preferred_element_type=F32)


def half_ffn(x, g, wg, wu, wd):
    m, d = x.shape
    f = wg.shape[1]
    tm = _row_tile(m)
    row = pl.BlockSpec((tm, d), lambda i: (i, 0))
    return pl.pallas_call(
        _ffn_kernel,
        grid=(m // tm,),
        in_specs=[row, _resident((1, d)), _resident((d, f)), _resident((d, f)), _resident((f, d))],
        out_specs=row,
        out_shape=jax.ShapeDtypeStruct((m, d), F32),
        compiler_params=_params("parallel"),
        name="half_ffn",
    )(x, g.reshape(1, d), wg, wu, wd)


def _rmsnorm_kernel(x_ref, g_ref, o_ref):
    o_ref[...] = _rms(x_ref[...], g_ref[...]).astype(o_ref.dtype)


def rmsnorm(x, g, dtype):
    m, d = x.shape
    tm = _row_tile(m)
    row = pl.BlockSpec((tm, d), lambda i: (i, 0))
    return pl.pallas_call(
        _rmsnorm_kernel,
        grid=(m // tm,),
        in_specs=[row, _resident((1, d))],
        out_specs=row,
        out_shape=jax.ShapeDtypeStruct((m, d), dtype),
        compiler_params=_params("parallel"),
        name="rmsnorm",
    )(x, g.reshape(1, d))


def _glu_kernel(y_ref, x_ref, g_ref, d_ref, wa_ref, wb_ref, o_ref):
    x = x_ref[...]
    u = _rms(x, g_ref[...])
    act = jax.nn.gelu(y_ref[...] + d_ref[...] * u).astype(BF16)
    a = jnp.dot(act, wa_ref[...], preferred_element_type=F32)
    b = jnp.dot(act, wb_ref[...], preferred_element_type=F32)
    o_ref[...] = x + a * jax.nn.sigmoid(b)


def ssm_glu(y, x, g_mix, d_skip, wa, wb):
    m, d = x.shape
    tm = _row_tile(m)
    row = pl.BlockSpec((tm, d), lambda i: (i, 0))
    return pl.pallas_call(
        _glu_kernel,
        grid=(m // tm,),
        in_specs=[row, row, _resident((1, d)), _resident((1, d)), _resident((d, d)), _resident((d, d))],
        out_specs=row,
        out_shape=jax.ShapeDtypeStruct((m, d), F32),
        compiler_params=_params("parallel"),
        name="ssm_glu",
    )(y, x, g_mix.reshape(1, d), d_skip.reshape(1, d), wa, wb)


def _kv_kernel(n_sum, x_ref, g_ref, wk_ref, wv_ref, k_ref, v_ref, kb_ref, vb_ref, *sum_ref):
    h = _rms(x_ref[...], g_ref[...]).astype(BF16)
    k = jnp.dot(h, wk_ref[...], preferred_element_type=F32)
    v = jnp.dot(h, wv_ref[...], preferred_element_type=F32)
    k_ref[...] = k
    v_ref[...] = v
    kb_ref[...] = k.astype(BF16)
    vb_ref[...] = v.astype(BF16)
    for r in range(n_sum):
        sum_ref[0][r] = jnp.sum(k[r * MOBA_BLOCK:(r + 1) * MOBA_BLOCK], axis=0, keepdims=True)


def shared_kv(x, g, wk, wv, with_block_sums):
    m, d = x.shape
    tm = _row_tile(m)
    n_sum = tm // MOBA_BLOCK if with_block_sums else 0
    row = pl.BlockSpec((tm, d), lambda i: (i, 0))
    out_specs = [row, row, row, row]
    out_shape = [jax.ShapeDtypeStruct((m, d), F32)] * 2 + [jax.ShapeDtypeStruct((m, d), BF16)] * 2
    if with_block_sums:
        assert tm % MOBA_BLOCK == 0
        out_specs.append(pl.BlockSpec((n_sum, 1, d), lambda i: (i, 0, 0)))
        out_shape.append(jax.ShapeDtypeStruct((m // MOBA_BLOCK, 1, d), F32))
    return pl.pallas_call(
        functools.partial(_kv_kernel, n_sum),
        grid=(m // tm,),
        in_specs=[row, _resident((1, d)), _resident((d, d)), _resident((d, d))],
        out_specs=out_specs,
        out_shape=out_shape,
        compiler_params=_params("parallel"),
        name="shared_kv",
    )(x, g.reshape(1, d), wk, wv)


def _proj_residual_kernel(a_ref, w_ref, x_ref, o_ref):
    o_ref[...] = x_ref[...] + jnp.dot(a_ref[...], w_ref[...], preferred_element_type=F32)


def proj_residual(a, w, x):
    m, d = x.shape
    tm = _row_tile(m)
    row = pl.BlockSpec((tm, d), lambda i: (i, 0))
    return pl.pallas_call(
        _proj_residual_kernel,
        grid=(m // tm,),
        in_specs=[row, _resident((d, d)), row],
        out_specs=row,
        out_shape=jax.ShapeDtypeStruct((m, d), F32),
        compiler_params=_params("parallel"),
        name="proj_residual",
    )(a, w, x)


def _norm_proj_kernel(x_ref, g_ref, w_ref, o_ref):
    h = _rms(x_ref[...], g_ref[...]).astype(BF16)
    o_ref[...] = jnp.dot(h, w_ref[...], preferred_element_type=F32)


def norm_proj(x, g, w):
    m, d = x.shape
    tm = _row_tile(m)
    row = pl.BlockSpec((tm, d), lambda i: (i, 0))
    return pl.pallas_call(
        _norm_proj_kernel,
        grid=(m // tm,),
        in_specs=[row, _resident((1, d)), _resident((d, d))],
        out_specs=row,
        out_shape=jax.ShapeDtypeStruct((m, d), F32),
        compiler_params=_params("parallel"),
        name="norm_proj",
    )(x, g.reshape(1, d), w)


def _ssm_weights_kernel(are_ref, aim_ref, ls_ref, bre_ref, bim_ref, cre_ref, cim_ref,
                        wt_ref, bst_ref, cst_ref, lb_ref):
    p = STATE_DIM
    lam_re = jnp.minimum(are_ref[0], -1e-4)
    lam_im = aim_ref[0]
    dt = jnp.exp(ls_ref[0])
    n_lag = SSM_CHUNK + 8
    lag = lax.broadcasted_iota(jnp.int32, (n_lag, p), 0).astype(F32)
    mag = jnp.exp(lag * (lam_re * dt))
    ang = lag * (lam_im * dt)
    pw_re = mag * jnp.cos(ang)
    pw_im = mag * jnp.sin(ang)
    lb_re = pw_re[1:2]
    lb_im = pw_im[1:2]
    den = lam_re * lam_re + lam_im * lam_im
    nr = lb_re - 1.0
    coef_re = (nr * lam_re + lb_im * lam_im) / den
    coef_im = (lb_im * lam_re - nr * lam_im) / den
    bt_re = bre_ref[0]
    bt_im = bim_ref[0]
    bb_re = coef_re * bt_re - coef_im * bt_im
    bb_im = coef_re * bt_im + coef_im * bt_re
    c_re = cre_ref[0]
    c_im = cim_ref[0]

    def c_pow(t):
        re = c_re * pw_re[t:t + 1] - c_im * pw_im[t:t + 1]
        im = c_re * pw_im[t:t + 1] + c_im * pw_re[t:t + 1]
        return jnp.concatenate([re, -im], axis=1)

    cp = [c_pow(t) for t in range(SSM_CHUNK + 1)]
    for t in range(SSM_CHUNK):
        cst_ref[0, t * GROUP_SIZE:(t + 1) * GROUP_SIZE, :] = cp[t + 1].astype(cst_ref.dtype)
    cpow0 = jnp.concatenate(cp[:SSM_CHUNK], axis=0)
    bb_cat = jnp.concatenate([bb_re, bb_im], axis=1)
    mt = lax.dot_general(bb_cat, cpow0, _NT, precision=lax.Precision.HIGHEST,
                         preferred_element_type=F32)
    lane = lax.broadcasted_iota(jnp.int32, mt.shape, 1)
    for s in range(SSM_CHUNK):
        shifted = mt if s == 0 else pltpu.roll(mt, s * GROUP_SIZE, axis=1)
        blk = jnp.where(lane >= s * GROUP_SIZE, shifted, 0.0)
        wt_ref[0, s * GROUP_SIZE:(s + 1) * GROUP_SIZE, :] = blk.astype(wt_ref.dtype)
        q = SSM_CHUNK - 1 - s
        re = bb_re * pw_re[q:q + 1] - bb_im * pw_im[q:q + 1]
        im = bb_re * pw_im[q:q + 1] + bb_im * pw_re[q:q + 1]
        bst_ref[0, s * GROUP_SIZE:(s + 1) * GROUP_SIZE, :] = (
            jnp.concatenate([re, im], axis=1).astype(bst_ref.dtype))
    t = SSM_CHUNK
    lb_ref[0, 0:1, :] = jnp.concatenate([pw_re[t:t + 1], pw_im[t:t + 1]], axis=1)
    lb_ref[0, 1:2, :] = jnp.concatenate([lb_re, lb_im], axis=1)
    lb_ref[0, 2:8, :] = jnp.zeros((6, 2 * p), F32)


def ssm_weights(a_re, a_im, log_step, b_re, b_im, c_re, c_im):
    g, p, gs = N_GROUPS, STATE_DIM, GROUP_SIZE
    row = lambda n: pl.BlockSpec((1, 1, n), lambda i: (i, 0, 0))
    mat = lambda r, c: pl.BlockSpec((1, r, c), lambda i: (i, 0, 0))
    return pl.pallas_call(
        _ssm_weights_kernel,
        grid=(g,),
        in_specs=[row(p), row(p), row(1), mat(gs, p), mat(gs, p), mat(gs, p), mat(gs, p)],
        out_specs=[mat(CHUNK_W, CHUNK_W), mat(CHUNK_W, 2 * p), mat(CHUNK_W, 2 * p), mat(8, 2 * p)],
        out_shape=[jax.ShapeDtypeStruct((g, CHUNK_W, CHUNK_W), BF16),
                   jax.ShapeDtypeStruct((g, CHUNK_W, 2 * p), BF16),
                   jax.ShapeDtypeStruct((g, CHUNK_W, 2 * p), BF16),
                   jax.ShapeDtypeStruct((g, 8, 2 * p), F32)],
        compiler_params=_params("parallel"),
        name="ssm_weights",
    )(a_re.reshape(g, 1, p), a_im.reshape(g, 1, p), log_step.reshape(g, 1, 1),
      b_re.transpose(0, 2, 1), b_im.transpose(0, 2, 1), c_re, c_im)


def _chunk_state_kernel(u_ref, bst_ref, e_ref):
    for gi in range(SSM_GROUP_TILE):
        e_ref[gi] = jnp.dot(u_ref[gi], bst_ref[gi], preferred_element_type=F32)


def chunk_states(u, bst):
    g, nc, w = u.shape
    gt = SSM_GROUP_TILE
    blk = lambda r, c: pl.BlockSpec((gt, r, c), lambda i: (i, 0, 0))
    return pl.pallas_call(
        _chunk_state_kernel,
        grid=(g // gt,),
        in_specs=[blk(nc, w), blk(w, 2 * STATE_DIM)],
        out_specs=blk(nc, 2 * STATE_DIM),
        out_shape=jax.ShapeDtypeStruct((g, nc, 2 * STATE_DIM), F32),
        compiler_params=_params("parallel"),
        name="ssm_chunk_states",
    )(u, bst)


def _scan_kernel(n_steps, e_ref, h0_ref, a_ref, hin_ref, hfin_ref, h_scr, hs_scr):
    half = STATE_DIM

    @pl.when(pl.program_id(1) == 0)
    def _():
        h_scr[...] = h0_ref[0]
        hs_scr[...] = pltpu.roll(h0_ref[0], half, axis=1)

    a_same = a_ref[0]
    a_cross = a_ref[1]
    a_cross_s = a_ref[2]

    def body(t, carry):
        h, hs = carry
        e = e_ref[t]
        es = pltpu.roll(e, half, axis=1)
        hin_ref[t] = h
        return a_same * h + a_cross * hs + e, a_same * hs + a_cross_s * h + es

    h, hs = lax.fori_loop(0, n_steps, body, (h_scr[...], hs_scr[...]), unroll=8)
    h_scr[...] = h
    hs_scr[...] = hs
    hfin_ref[0] = h


def chunk_scan(e, h0, a):
    nc, g, w = e.shape
    n_seq = h0.shape[0]
    per_seq = nc // n_seq
    ct = SCAN_TILE if per_seq % SCAN_TILE == 0 else per_seq
    n_ct = per_seq // ct
    tile = pl.BlockSpec((ct, g, w), lambda b, c: (b * n_ct + c, 0, 0))
    seq = pl.BlockSpec((1, g, w), lambda b, c: (b, 0, 0))
    return pl.pallas_call(
        functools.partial(_scan_kernel, ct),
        grid=(n_seq, n_ct),
        in_specs=[tile, seq, _resident((3, g, w))],
        out_specs=[tile, seq],
        out_shape=[jax.ShapeDtypeStruct((nc, g, w), F32), jax.ShapeDtypeStruct((n_seq, g, w), F32)],
        scratch_shapes=[pltpu.VMEM((g, w), F32), pltpu.VMEM((g, w), F32)],
        compiler_params=_params("parallel", "arbitrary"),
        name="ssm_chunk_scan",
    )(e, h0, a)


def _chunk_out_kernel(u_ref, hin_ref, wt_ref, cst_ref, y_ref):
    for gi in range(SSM_GROUP_TILE):
        intra = jnp.dot(u_ref[gi], wt_ref[gi], preferred_element_type=F32)
        inter = lax.dot_general(hin_ref[gi].astype(BF16), cst_ref[gi], _NT, preferred_element_type=F32)
        y_ref[gi] = intra + inter


def chunk_outputs(u, hin, wt, cst):
    g, nc, w = u.shape
    gt = SSM_GROUP_TILE
    blk = lambda r, c: pl.BlockSpec((gt, r, c), lambda i: (i, 0, 0))
    return pl.pallas_call(
        _chunk_out_kernel,
        grid=(g // gt,),
        in_specs=[blk(nc, w), blk(nc, 2 * STATE_DIM), blk(w, w), blk(w, 2 * STATE_DIM)],
        out_specs=blk(nc, w),
        out_shape=jax.ShapeDtypeStruct((g, nc, w), F32),
        compiler_params=_params("parallel"),
        name="ssm_chunk_outputs",
    )(u, hin, wt, cst)


def s5_prompt(u_bf, n_seq, wt, bst, cst, lb):
    m, d = u_bf.shape
    g, gs, p = N_GROUPS, GROUP_SIZE, STATE_DIM
    nc = m // SSM_CHUNK
    u = u_bf.reshape(nc, SSM_CHUNK, g, gs).transpose(2, 0, 1, 3).reshape(g, nc, CHUNK_W)
    e = chunk_states(u, bst).transpose(1, 0, 2)
    lt_re, lt_im = lb[:, 0, :p], lb[:, 0, p:]
    a = jnp.stack([jnp.concatenate([lt_re, lt_re], axis=1),
                   jnp.concatenate([-lt_im, lt_im], axis=1),
                   jnp.concatenate([lt_im, -lt_im], axis=1)])
    h0 = jnp.zeros((n_seq, g, 2 * p), F32)
    hin, hfin = chunk_scan(e, h0, a)
    y = chunk_outputs(u, hin.transpose(1, 0, 2), wt, cst)
    y = y.reshape(g, nc, SSM_CHUNK, gs).transpose(1, 2, 0, 3).reshape(m, d)
    return y, hfin[:, :, :p], hfin[:, :, p:]


def _ssm_step_kernel(u_ref, hre_ref, him_ref, m0_ref, bre_ref, bim_ref, cre_ref, cim_ref,
                     lre_ref, lim_ref, y_ref, ore_ref, oim_ref):
    u = u_ref[...]
    h_re, h_im = hre_ref[...], him_ref[...]
    l_re, l_im = lre_ref[...], lim_ref[...]
    bmm = lambda a, b: jnp.einsum("gbk,gkn->gbn", a, b, preferred_element_type=F32)
    bmm_nt = lambda a, b: jnp.einsum("gbk,gnk->gbn", a, b, preferred_element_type=F32)
    ore_ref[...] = l_re * h_re - l_im * h_im + bmm(u, bre_ref[...].astype(F32))
    oim_ref[...] = l_re * h_im + l_im * h_re + bmm(u, bim_ref[...].astype(F32))
    y_ref[...] = (bmm(u, m0_ref[...].astype(F32)) + bmm_nt(h_re, cre_ref[...].astype(F32))
                  + bmm_nt(h_im, cim_ref[...].astype(F32)))


def s5_step(u, h_re, h_im, wt, bst, cst, lb):
    b, d = u.shape
    g, gs, p = N_GROUPS, GROUP_SIZE, STATE_DIM
    last = (SSM_CHUNK - 1) * gs
    y, o_re, o_im = pl.pallas_call(
        _ssm_step_kernel,
        out_shape=[jax.ShapeDtypeStruct((g, b, gs), F32),
                   jax.ShapeDtypeStruct((g, b, p), F32), jax.ShapeDtypeStruct((g, b, p), F32)],
        compiler_params=pltpu.CompilerParams(vmem_limit_bytes=VMEM_LIMIT),
        name="ssm_step",
    )(u.reshape(b, g, gs).transpose(1, 0, 2), h_re.transpose(1, 0, 2), h_im.transpose(1, 0, 2),
      wt[:, :gs, :gs], bst[:, last:, :p], bst[:, last:, p:], cst[:, :gs, :p], cst[:, :gs, p:],
      lb[:, 1:2, :p], lb[:, 1:2, p:])
    return y.transpose(1, 0, 2).reshape(b, d), o_re.transpose(1, 0, 2), o_im.transpose(1, 0, 2)


def _select_top_blocks(gate, blk, n_invalid):
    bias = jnp.full(gate.shape, MASK_VALUE, F32)
    for _ in range(TOP_K):
        m = jnp.max(gate, axis=1, keepdims=True)
        first = jnp.min(jnp.where(gate == m, blk, n_invalid), axis=1, keepdims=True)
        hit = blk == first
        bias = jnp.where(hit, jnp.where(m > -jnp.inf, 0.0, bias), bias)
        gate = jnp.where(hit, -jnp.inf, gate)
    return bias


def _query_select_kernel(x_ref, g_ref, wq_ref, ks_ref, o_ref):
    own = pl.program_id(1)
    h = _rms(x_ref[...], g_ref[...]).astype(BF16)
    q = jnp.dot(h, wq_ref[...], preferred_element_type=F32)
    qb = q.astype(BF16)
    means = (ks_ref[0] * (1.0 / MOBA_BLOCK)).astype(BF16)
    blk = lax.broadcasted_iota(jnp.int32, (q.shape[0], LANES), 1)
    for hh in range(N_HEADS):
        cols = slice(hh * HEAD_DIM, (hh + 1) * HEAD_DIM)
        gate = lax.dot_general(qb[:, cols], means[:, cols], _NT, preferred_element_type=F32)
        gate = jnp.where(blk < own, gate, -jnp.inf)
        o_ref[0, hh, :, :HEAD_DIM] = (q[:, cols] * ATTN_SCALE).astype(BF16)
        o_ref[0, hh, :, HEAD_DIM:] = _select_top_blocks(gate, blk, LANES).astype(BF16)


def query_select(x, g, wq, block_sums, n_seq):
    m, d = x.shape
    seq = m // n_seq
    nb = seq // MOBA_BLOCK
    assert seq % MOBA_BLOCK == 0 and nb <= LANES
    sums = jnp.pad(block_sums.reshape(n_seq, nb, d), ((0, 0), (0, LANES - nb), (0, 0)))
    return pl.pallas_call(
        _query_select_kernel,
        grid=(n_seq, nb),
        in_specs=[pl.BlockSpec((MOBA_BLOCK, d), lambda b, i: (b * nb + i, 0)),
                  _resident((1, d)), _resident((d, d)),
                  pl.BlockSpec((1, LANES, d), lambda b, i: (b, 0, 0))],
        out_specs=pl.BlockSpec((1, N_HEADS, MOBA_BLOCK, 2 * HEAD_DIM), lambda b, i: (b, 0, i, 0)),
        out_shape=jax.ShapeDtypeStruct((n_seq, N_HEADS, seq, 2 * HEAD_DIM), BF16),
        compiler_params=_params("parallel", "parallel"),
        name="moba_query_select",
    )(x, g.reshape(1, d), wq, sums)


def _moba_kernel(q_ref, k_ref, v_ref, o_ref):
    i = pl.program_id(2)
    tq = MOBA_BLOCK
    own = pl.multiple_of(i * tq, tq)
    s = lax.dot_general(q_ref[0, 0, :, :HEAD_DIM], k_ref[0, pl.ds(own, tq), :], _NT,
                        preferred_element_type=F32)
    row = lax.broadcasted_iota(jnp.int32, s.shape, 0)
    col = lax.broadcasted_iota(jnp.int32, s.shape, 1)
    s = jnp.where(col <= row, s, MASK_VALUE)
    m = jnp.max(s, axis=1, keepdims=True)
    p = jnp.exp(s - m)
    l = jnp.sum(p, axis=1, keepdims=True)
    acc = jnp.dot(p.astype(BF16), v_ref[0, pl.ds(own, tq), :], preferred_element_type=F32)
    lane = lax.broadcasted_iota(jnp.int32, (tq, HEAD_DIM), 1)

    def body(n, carry):
        m, l, acc = carry
        off = pl.multiple_of(n * tq, tq)
        one_hot = jnp.where(lane == n, 1.0, 0.0).astype(BF16)
        k_ext = jnp.concatenate([k_ref[0, pl.ds(off, tq), :], one_hot], axis=1)
        s = lax.dot_general(q_ref[0, 0], k_ext, _NT, preferred_element_type=F32)
        m_new = jnp.maximum(m, jnp.max(s, axis=1, keepdims=True))
        alpha = jnp.exp(m - m_new)
        p = jnp.exp(s - m_new)
        l = alpha * l + jnp.sum(p, axis=1, keepdims=True)
        acc = alpha * acc + jnp.dot(p.astype(BF16), v_ref[0, pl.ds(off, tq), :],
                                    preferred_element_type=F32)
        return m_new, l, acc

    m, l, acc = lax.fori_loop(0, i, body, (m, l, acc))
    o_ref[0] = (acc / l).astype(o_ref.dtype)


def moba_prompt(q_ext, k_bf, v_bf):
    n_seq, _, seq, _ = q_ext.shape
    d = k_bf.shape[-1]
    nb = seq // MOBA_BLOCK
    kv = pl.BlockSpec((1, seq, HEAD_DIM), lambda b, h, i: (b, 0, h))
    out = pl.pallas_call(
        _moba_kernel,
        grid=(n_seq, N_HEADS, nb),
        in_specs=[pl.BlockSpec((1, 1, MOBA_BLOCK, 2 * HEAD_DIM), lambda b, h, i: (b, h, i, 0)), kv, kv],
        out_specs=pl.BlockSpec((1, MOBA_BLOCK, HEAD_DIM), lambda b, h, i: (b, i, h)),
        out_shape=jax.ShapeDtypeStruct((n_seq, seq, d), BF16),
        compiler_params=_params("parallel", "parallel", "arbitrary"),
        name="moba_prompt",
    )(q_ext, k_bf, v_bf)
    return out.reshape(n_seq * seq, d)


def _page_sum_kernel(pt_ref, *refs):
    pages, o_ref = refs[:PAGES_PER_STEP], refs[PAGES_PER_STEP]
    per_block = MOBA_BLOCK // PAGE_SIZE
    for r in range(PAGES_PER_STEP // per_block):
        total = jnp.sum(pages[r * per_block][0], axis=0, keepdims=True)
        for j in range(1, per_block):
            total = total + jnp.sum(pages[r * per_block + j][0], axis=0, keepdims=True)
        o_ref[0, r:r + 1, :] = total


def block_sums_paged(cache_k, page_table):
    n_pool, ps, d = cache_k.shape
    b, n_pages = page_table.shape
    per_block = MOBA_BLOCK // ps
    steps = n_pages // PAGES_PER_STEP
    assert ps == PAGE_SIZE and n_pages % PAGES_PER_STEP == 0
    page = lambda r: pl.BlockSpec((1, ps, d), lambda s, t, pt: (pt[s, t * PAGES_PER_STEP + r], 0, 0))
    rows = PAGES_PER_STEP // per_block
    return pl.pallas_call(
        _page_sum_kernel,
        grid_spec=pltpu.PrefetchScalarGridSpec(
            num_scalar_prefetch=1,
            grid=(b, steps),
            in_specs=[page(r) for r in range(PAGES_PER_STEP)],
            out_specs=pl.BlockSpec((1, rows, d), lambda s, t, pt: (s, t, 0)),
        ),
        out_shape=jax.ShapeDtypeStruct((b, n_pages // per_block, d), F32),
        compiler_params=_params("parallel", "parallel"),
        name="cache_block_sums",
    )(page_table, *([cache_k] * PAGES_PER_STEP))


def _sample_select_kernel(q_ref, sums_ref, o_ref):
    q = q_ref[0]
    prod = q * (sums_ref[0] * (1.0 / MOBA_BLOCK))
    nb = prod.shape[0]
    lane = lax.broadcasted_iota(jnp.int32, (nb, LANES), 1)
    gate = jnp.full((nb, LANES), -jnp.inf, F32)
    for hh in range(N_HEADS):
        g_h = jnp.sum(prod[:, hh * HEAD_DIM:(hh + 1) * HEAD_DIM], axis=1, keepdims=True)
        gate = jnp.where(lane == hh, g_h, gate)
    blk = lax.broadcasted_iota(jnp.int32, (nb, LANES), 0)
    rows = []
    for _ in range(TOP_K):
        m = jnp.max(gate, axis=0, keepdims=True)
        first = jnp.min(jnp.where(gate == m, blk, nb), axis=0, keepdims=True)
        rows.append(jnp.minimum(first, nb - 1))
        gate = jnp.where(blk == first, -jnp.inf, gate)
    rows.append(jnp.zeros((8 - TOP_K, LANES), jnp.int32))
    o_ref[0] = jnp.concatenate(rows, axis=0)


def sample_select(q, sums):
    b, d = q.shape
    nb = sums.shape[1]
    assert nb >= TOP_K
    out = pl.pallas_call(
        _sample_select_kernel,
        grid=(b,),
        in_specs=[pl.BlockSpec((1, 1, d), lambda s: (s, 0, 0)),
                  pl.BlockSpec((1, nb, d), lambda s: (s, 0, 0))],
        out_specs=pl.BlockSpec((1, 8, LANES), lambda s: (s, 0, 0)),
        out_shape=jax.ShapeDtypeStruct((b, 8, LANES), jnp.int32),
        compiler_params=_params("parallel"),
        name="sample_select",
    )(q.reshape(b, 1, d), sums)
    return out[:, :TOP_K, :N_HEADS].transpose(0, 2, 1)


def _sample_attn_kernel(n_pages, sel_ref, pt_ref, q_ref, kn_ref, vn_ref, *refs):
    k_pages, v_pages, o_ref = refs[:n_pages], refs[n_pages:2 * n_pages], refs[2 * n_pages]
    q = q_ref[0, 0] * ATTN_SCALE
    scores = [lax.dot_general(q, kp[0], _NT, preferred_element_type=F32) for kp in k_pages]
    s_new = jnp.sum(q * kn_ref[0, 0], axis=1, keepdims=True)
    m = s_new
    for s in scores:
        m = jnp.maximum(m, jnp.max(s, axis=1, keepdims=True))
    p_new = jnp.exp(s_new - m)
    l = p_new
    acc = p_new * vn_ref[0, 0]
    for s, vp in zip(scores, v_pages):
        p = jnp.exp(s - m)
        l = l + jnp.sum(p, axis=1, keepdims=True)
        acc = acc + jnp.dot(p, vp[0], preferred_element_type=F32)
    o_ref[0, 0] = acc / l


def moba_sample(q, k_new, v_new, cache_k, cache_v, page_table, sel):
    b, d = q.shape
    ps = cache_k.shape[1]
    per_block = MOBA_BLOCK // ps
    n_pages = TOP_K * per_block

    def page(j):
        def index(s, h, sel_ref, pt_ref):
            return pt_ref[s, sel_ref[s, h, j // per_block] * per_block + j % per_block], 0, h
        return pl.BlockSpec((1, ps, HEAD_DIM), index)

    head = pl.BlockSpec((1, 1, 1, HEAD_DIM), lambda s, h, sel_ref, pt_ref: (s, h, 0, 0))
    as_heads = lambda a: a.reshape(b, N_HEADS, 1, HEAD_DIM)
    out = pl.pallas_call(
        functools.partial(_sample_attn_kernel, n_pages),
        grid_spec=pltpu.PrefetchScalarGridSpec(
            num_scalar_prefetch=2,
            grid=(b, N_HEADS),
            in_specs=[head, head, head] + [page(j) for j in range(n_pages)] * 2,
            out_specs=head,
        ),
        out_shape=jax.ShapeDtypeStruct((b, N_HEADS, 1, HEAD_DIM), F32),
        compiler_params=_params("parallel", "parallel"),
        name="moba_sample",
    )(sel, page_table, as_heads(q), as_heads(k_new), as_heads(v_new),
      *([cache_k] * n_pages), *([cache_v] * n_pages))
    return out.reshape(b, d)


def kernel(x_prompt, x_sample, state_ssm_re, state_ssm_im, cache_k, cache_v, page_table,
           norm_ffn1, w_ffn1_gate, w_ffn1_up, w_ffn1_down, norm_mix,
           norm_ffn2, w_ffn2_gate, w_ffn2_up, w_ffn2_down,
           ssm_a_re, ssm_a_im, ssm_log_step, ssm_b_re, ssm_b_im, ssm_c_re, ssm_c_im,
           ssm_d, ssm_w_glu_a, ssm_w_glu_b,
           norm_kv, w_k, w_v, attn_w_q, attn_w_o, norm_final):
    n_seq, seq, d = x_prompt.shape
    n_dec, dec_seq, _ = x_sample.shape
    assert dec_seq == 1 and norm_ffn1.shape[0] == 2 and ssm_a_re.shape[0] == 1
    n_pool, page_size, n_heads, head_dim = cache_k.shape
    assert (page_table.shape[1] * page_size) % MOBA_BLOCK == 0

    bf = lambda w: w.astype(BF16)
    ffn1 = [(norm_ffn1[l], bf(w_ffn1_gate[l]), bf(w_ffn1_up[l]), bf(w_ffn1_down[l])) for l in range(2)]
    ffn2 = [(norm_ffn2[l], bf(w_ffn2_gate[l]), bf(w_ffn2_up[l]), bf(w_ffn2_down[l])) for l in range(2)]
    glu_a, glu_b = bf(ssm_w_glu_a[0]), bf(ssm_w_glu_b[0])
    wk, wv, wq, wo = bf(w_k), bf(w_v), bf(attn_w_q[0]), bf(attn_w_o[0])
    ssm_ops = ssm_weights(ssm_a_re[0], ssm_a_im[0], ssm_log_step[0], ssm_b_re[0], ssm_b_im[0],
                          ssm_c_re[0], ssm_c_im[0])

    def layer_a(x, mixer):
        x = half_ffn(x, *ffn1[0])
        y, h_re, h_im = mixer(x)
        x = ssm_glu(y, x, norm_mix[0], ssm_d[0], glu_a, glu_b)
        return half_ffn(x, *ffn2[0]), h_re, h_im

    xp = x_prompt.reshape(n_seq * seq, d)
    xp, p_re, p_im = layer_a(
        xp, lambda x: s5_prompt(rmsnorm(x, norm_mix[0], BF16), n_seq, *ssm_ops))
    k_p, v_p, kb_p, vb_p, ksum_p = shared_kv(xp, norm_kv, wk, wv, True)
    xp = half_ffn(xp, *ffn1[1])
    q_ext = query_select(xp, norm_mix[1], wq, ksum_p, n_seq)
    attn_p = moba_prompt(q_ext, kb_p.reshape(n_seq, seq, d), vb_p.reshape(n_seq, seq, d))
    xp = proj_residual(attn_p, wo, xp)
    xp = half_ffn(xp, *ffn2[1])
    y_prompt = rmsnorm(xp, norm_final, F32).reshape(n_seq, seq, d)

    xs = x_sample.reshape(n_dec, d)
    xs, s_re, s_im = layer_a(
        xs, lambda x: s5_step(rmsnorm(x, norm_mix[0], F32), state_ssm_re[0], state_ssm_im[0], *ssm_ops))
    k_s, v_s, _, _ = shared_kv(xs, norm_kv, wk, wv, False)
    xs = half_ffn(xs, *ffn1[1])
    q_s = norm_proj(xs, norm_mix[1], wq)
    ck = cache_k.reshape(n_pool, page_size, n_heads * head_dim)
    cv = cache_v.reshape(n_pool, page_size, n_heads * head_dim)
    sel = sample_select(q_s, block_sums_paged(ck, page_table))
    attn_s = moba_sample(q_s, k_s, v_s, ck, cv, page_table, sel)
    xs = proj_residual(attn_s.astype(BF16), wo, xs)
    xs = half_ffn(xs, *ffn2[1])
    y_sample = rmsnorm(xs, norm_final, F32).reshape(n_dec, 1, d)

    heads = lambda a, b, s: a.reshape(b, s, n_heads, head_dim)
    return (y_prompt, y_sample, p_re[None], p_im[None], s_re[None], s_im[None],
            heads(k_p, n_seq, seq), heads(v_p, n_seq, seq), heads(k_s, n_dec, 1), heads(v_s, n_dec, 1))
```

```python
import functools

import jax
import jax.numpy as jnp
from jax import lax
from jax.experimental import pallas as pl
from jax.experimental.pallas import tpu as pltpu

F32 = jnp.float32
BF16 = jnp.bfloat16

D_MODEL = 1024
N_GROUPS = 64
GROUP_SIZE = 16
STATE_DIM = 64
N_HEADS = 8
HEAD_DIM = 128
MOBA_BLOCK = 256
TOP_K = 3
PAGE_SIZE = 128
RMS_EPS = 1e-6
ATTN_SCALE = HEAD_DIM ** -0.5
LOG2_E = 1.4426950408889634
MASK_VALUE = -1e30

SSM_CHUNK = 16
CHUNK_W = SSM_CHUNK * GROUP_SIZE
LANES = 128
ROW_TILE = 512
SSM_GROUP_TILE = 4
SCAN_TILE = 128
PAGES_PER_STEP = 16
MOBA_HEADS_PER_STEP = 8
VMEM_LIMIT = 56 * 1024 * 1024

_NT = (((1,), (1,)), ((), ()))


def _params(*sem):
    return pltpu.CompilerParams(dimension_semantics=sem, vmem_limit_bytes=VMEM_LIMIT)


def _resident(shape):
    zeros = (0,) * len(shape)
    return pl.BlockSpec(shape, lambda *_: zeros, pipeline_mode=pl.Buffered(1))


def _rms(x, g):
    inv = lax.rsqrt(jnp.mean(x * x, axis=-1, keepdims=True) + RMS_EPS)
    return x * inv * g


def _row_tile(m):
    return ROW_TILE if m % ROW_TILE == 0 else m


def _ffn_kernel(x_ref, g_ref, wg_ref, wu_ref, wd_ref, o_ref):
    x = x_ref[...]
    h = _rms(x, g_ref[...]).astype(BF16)
    a = jnp.dot(h, wg_ref[...], preferred_element_type=F32)
    b = jnp.dot(h, wu_ref[...], preferred_element_type=F32)
    act = (jax.nn.silu(a) * b).astype(BF16)
    o_ref[...] = x + 0.5 * jnp.dot(act, wd_ref[...], preferred_element_type=F32)


def half_ffn(x, g, wg, wu, wd):
    m, d = x.shape
    f = wg.shape[1]
    tm = _row_tile(m)
    row = pl.BlockSpec((tm, d), lambda i: (i, 0))
    return pl.pallas_call(
        _ffn_kernel,
        grid=(m // tm,),
        in_specs=[row, _resident((1, d)), _resident((d, f)), _resident((d, f)), _resident((f, d))],
        out_specs=row,
        out_shape=jax.ShapeDtypeStruct((m, d), F32),
        compiler_params=_params("parallel"),
        name="half_ffn",
    )(x, g.reshape(1, d), wg, wu, wd)


def _rmsnorm_kernel(x_ref, g_ref, o_ref):
    o_ref[...] = _rms(x_ref[...], g_ref[...]).astype(o_ref.dtype)


def rmsnorm(x, g, dtype):
    m, d = x.shape
    tm = _row_tile(m)
    row = pl.BlockSpec((tm, d), lambda i: (i, 0))
    return pl.pallas_call(
        _rmsnorm_kernel,
        grid=(m // tm,),
        in_specs=[row, _resident((1, d))],
        out_specs=row,
        out_shape=jax.ShapeDtypeStruct((m, d), dtype),
        compiler_params=_params("parallel"),
        name="rmsnorm",
    )(x, g.reshape(1, d))


def _glu_kernel(y_ref, x_ref, g_ref, d_ref, wa_ref, wb_ref, o_ref):
    x = x_ref[...]
    u = _rms(x, g_ref[...])
    act = jax.nn.gelu(y_ref[...] + d_ref[...] * u).astype(BF16)
    a = jnp.dot(act, wa_ref[...], preferred_element_type=F32)
    b = jnp.dot(act, wb_ref[...], preferred_element_type=F32)
    o_ref[...] = x + a * jax.nn.sigmoid(b)


def ssm_glu(y, x, g_mix, d_skip, wa, wb):
    m, d = x.shape
    tm = _row_tile(m)
    row = pl.BlockSpec((tm, d), lambda i: (i, 0))
    return pl.pallas_call(
        _glu_kernel,
        grid=(m // tm,),
        in_specs=[row, row, _resident((1, d)), _resident((1, d)), _resident((d, d)), _resident((d, d))],
        out_specs=row,
        out_shape=jax.ShapeDtypeStruct((m, d), F32),
        compiler_params=_params("parallel"),
        name="ssm_glu",
    )(y, x, g_mix.reshape(1, d), d_skip.reshape(1, d), wa, wb)


def _kv_kernel(n_sum, x_ref, g_ref, wk_ref, wv_ref, k_ref, v_ref, kb_ref, vb_ref, *sum_ref):
    h = _rms(x_ref[...], g_ref[...]).astype(BF16)
    k = jnp.dot(h, wk_ref[...], preferred_element_type=F32)
    v = jnp.dot(h, wv_ref[...], preferred_element_type=F32)
    k_ref[...] = k
    v_ref[...] = v
    kb_ref[...] = k.astype(BF16)
    vb_ref[...] = v.astype(BF16)
    for r in range(n_sum):
        sum_ref[0][r] = jnp.sum(k[r * MOBA_BLOCK:(r + 1) * MOBA_BLOCK], axis=0, keepdims=True)


def shared_kv(x, g, wk, wv, with_block_sums):
    m, d = x.shape
    tm = _row_tile(m)
    n_sum = tm // MOBA_BLOCK if with_block_sums else 0
    row = pl.BlockSpec((tm, d), lambda i: (i, 0))
    out_specs = [row, row, row, row]
    out_shape = [jax.ShapeDtypeStruct((m, d), F32)] * 2 + [jax.ShapeDtypeStruct((m, d), BF16)] * 2
    if with_block_sums:
        assert tm % MOBA_BLOCK == 0
        out_specs.append(pl.BlockSpec((n_sum, 1, d), lambda i: (i, 0, 0)))
        out_shape.append(jax.ShapeDtypeStruct((m // MOBA_BLOCK, 1, d), F32))
    return pl.pallas_call(
        functools.partial(_kv_kernel, n_sum),
        grid=(m // tm,),
        in_specs=[row, _resident((1, d)), _resident((d, d)), _resident((d, d))],
        out_specs=out_specs,
        out_shape=out_shape,
        compiler_params=_params("parallel"),
        name="shared_kv",
    )(x, g.reshape(1, d), wk, wv)


def _proj_residual_kernel(a_ref, w_ref, x_ref, o_ref):
    o_ref[...] = x_ref[...] + jnp.dot(a_ref[...], w_ref[...], preferred_element_type=F32)


def proj_residual(a, w, x):
    m, d = x.shape
    tm = _row_tile(m)
    row = pl.BlockSpec((tm, d), lambda i: (i, 0))
    return pl.pallas_call(
        _proj_residual_kernel,
        grid=(m // tm,),
        in_specs=[row, _resident((d, d)), row],
        out_specs=row,
        out_shape=jax.ShapeDtypeStruct((m, d), F32),
        compiler_params=_params("parallel"),
        name="proj_residual",
    )(a, w, x)


def _norm_proj_kernel(x_ref, g_ref, w_ref, o_ref):
    h = _rms(x_ref[...], g_ref[...]).astype(BF16)
    o_ref[...] = jnp.dot(h, w_ref[...], preferred_element_type=F32)


def norm_proj(x, g, w):
    m, d = x.shape
    tm = _row_tile(m)
    row = pl.BlockSpec((tm, d), lambda i: (i, 0))
    return pl.pallas_call(
        _norm_proj_kernel,
        grid=(m // tm,),
        in_specs=[row, _resident((1, d)), _resident((d, d))],
        out_specs=row,
        out_shape=jax.ShapeDtypeStruct((m, d), F32),
        compiler_params=_params("parallel"),
        name="norm_proj",
    )(x, g.reshape(1, d), w)


def _ssm_weights_kernel(are_ref, aim_ref, ls_ref, bre_ref, bim_ref, cre_ref, cim_ref,
                        wt_ref, bst_ref, cst_ref, lb_ref):
    p = STATE_DIM
    lam_re = jnp.minimum(are_ref[0], -1e-4)
    lam_im = aim_ref[0]
    dt = jnp.exp(ls_ref[0])
    n_lag = SSM_CHUNK + 8
    lag = lax.broadcasted_iota(jnp.int32, (n_lag, p), 0).astype(F32)
    mag = jnp.exp(lag * (lam_re * dt))
    ang = lag * (lam_im * dt)
    pw_re = mag * jnp.cos(ang)
    pw_im = mag * jnp.sin(ang)
    lb_re = pw_re[1:2]
    lb_im = pw_im[1:2]
    den = lam_re * lam_re + lam_im * lam_im
    nr = lb_re - 1.0
    coef_re = (nr * lam_re + lb_im * lam_im) / den
    coef_im = (lb_im * lam_re - nr * lam_im) / den
    bt_re = bre_ref[0]
    bt_im = bim_ref[0]
    bb_re = coef_re * bt_re - coef_im * bt_im
    bb_im = coef_re * bt_im + coef_im * bt_re
    c_re = cre_ref[0]
    c_im = cim_ref[0]

    def c_pow(t):
        re = c_re * pw_re[t:t + 1] - c_im * pw_im[t:t + 1]
        im = c_re * pw_im[t:t + 1] + c_im * pw_re[t:t + 1]
        return jnp.concatenate([re, -im], axis=1)

    cp = [c_pow(t) for t in range(SSM_CHUNK + 1)]
    for t in range(SSM_CHUNK):
        cst_ref[0, t * GROUP_SIZE:(t + 1) * GROUP_SIZE, :] = cp[t + 1].astype(cst_ref.dtype)
    cpow0 = jnp.concatenate(cp[:SSM_CHUNK], axis=0)
    bb_cat = jnp.concatenate([bb_re, bb_im], axis=1)
    mt = lax.dot_general(bb_cat, cpow0, _NT, precision=lax.Precision.HIGHEST,
                         preferred_element_type=F32)
    lane = lax.broadcasted_iota(jnp.int32, mt.shape, 1)
    for s in range(SSM_CHUNK):
        shifted = mt if s == 0 else pltpu.roll(mt, s * GROUP_SIZE, axis=1)
        blk = jnp.where(lane >= s * GROUP_SIZE, shifted, 0.0)
        wt_ref[0, s * GROUP_SIZE:(s + 1) * GROUP_SIZE, :] = blk.astype(wt_ref.dtype)
        q = SSM_CHUNK - 1 - s
        re = bb_re * pw_re[q:q + 1] - bb_im * pw_im[q:q + 1]
        im = bb_re * pw_im[q:q + 1] + bb_im * pw_re[q:q + 1]
        bst_ref[0, s * GROUP_SIZE:(s + 1) * GROUP_SIZE, :] = (
            jnp.concatenate([re, im], axis=1).astype(bst_ref.dtype))
    t = SSM_CHUNK
    lb_ref[0, 0:1, :] = jnp.concatenate([pw_re[t:t + 1], pw_im[t:t + 1]], axis=1)
    lb_ref[0, 1:2, :] = jnp.concatenate([lb_re, lb_im], axis=1)
    lb_ref[0, 2:8, :] = jnp.zeros((6, 2 * p), F32)


def ssm_weights(a_re, a_im, log_step, b_re, b_im, c_re, c_im):
    g, p, gs = N_GROUPS, STATE_DIM, GROUP_SIZE
    row = lambda n: pl.BlockSpec((1, 1, n), lambda i: (i, 0, 0))
    mat = lambda r, c: pl.BlockSpec((1, r, c), lambda i: (i, 0, 0))
    return pl.pallas_call(
        _ssm_weights_kernel,
        grid=(g,),
        in_specs=[row(p), row(p), row(1), mat(gs, p), mat(gs, p), mat(gs, p), mat(gs, p)],
        out_specs=[mat(CHUNK_W, CHUNK_W), mat(CHUNK_W, 2 * p), mat(CHUNK_W, 2 * p), mat(8, 2 * p)],
        out_shape=[jax.ShapeDtypeStruct((g, CHUNK_W, CHUNK_W), BF16),
                   jax.ShapeDtypeStruct((g, CHUNK_W, 2 * p), BF16),
                   jax.ShapeDtypeStruct((g, CHUNK_W, 2 * p), BF16),
                   jax.ShapeDtypeStruct((g, 8, 2 * p), F32)],
        compiler_params=_params("parallel"),
        name="ssm_weights",
    )(a_re.reshape(g, 1, p), a_im.reshape(g, 1, p), log_step.reshape(g, 1, 1),
      b_re.transpose(0, 2, 1), b_im.transpose(0, 2, 1), c_re, c_im)


def _chunk_state_kernel(u_ref, bst_ref, e_ref):
    for gi in range(SSM_GROUP_TILE):
        e_ref[gi] = jnp.dot(u_ref[gi], bst_ref[gi], preferred_element_type=F32)


def chunk_states(u, bst):
    g, nc, w = u.shape
    gt = SSM_GROUP_TILE
    blk = lambda r, c: pl.BlockSpec((gt, r, c), lambda i: (i, 0, 0))
    return pl.pallas_call(
        _chunk_state_kernel,
        grid=(g // gt,),
        in_specs=[blk(nc, w), blk(w, 2 * STATE_DIM)],
        out_specs=blk(nc, 2 * STATE_DIM),
        out_shape=jax.ShapeDtypeStruct((g, nc, 2 * STATE_DIM), F32),
        compiler_params=_params("parallel"),
        name="ssm_chunk_states",
    )(u, bst)


def _scan_kernel(n_steps, e_ref, h0_ref, a_ref, hin_ref, hfin_ref, h_scr, hs_scr):
    half = STATE_DIM

    @pl.when(pl.program_id(1) == 0)
    def _():
        h_scr[...] = h0_ref[0]
        hs_scr[...] = pltpu.roll(h0_ref[0], half, axis=1)

    a_same = a_ref[0]
    a_cross = a_ref[1]
    a_cross_s = a_ref[2]

    def body(t, carry):
        h, hs = carry
        e = e_ref[t]
        es = pltpu.roll(e, half, axis=1)
        hin_ref[t] = h
        return a_same * h + a_cross * hs + e, a_same * hs + a_cross_s * h + es

    h, hs = lax.fori_loop(0, n_steps, body, (h_scr[...], hs_scr[...]), unroll=8)
    h_scr[...] = h
    hs_scr[...] = hs
    hfin_ref[0] = h


def chunk_scan(e, h0, a):
    nc, g, w = e.shape
    n_seq = h0.shape[0]
    per_seq = nc // n_seq
    ct = SCAN_TILE if per_seq % SCAN_TILE == 0 else per_seq
    n_ct = per_seq // ct
    tile = pl.BlockSpec((ct, g, w), lambda b, c: (b * n_ct + c, 0, 0))
    seq = pl.BlockSpec((1, g, w), lambda b, c: (b, 0, 0))
    return pl.pallas_call(
        functools.partial(_scan_kernel, ct),
        grid=(n_seq, n_ct),
        in_specs=[tile, seq, _resident((3, g, w))],
        out_specs=[tile, seq],
        out_shape=[jax.ShapeDtypeStruct((nc, g, w), F32), jax.ShapeDtypeStruct((n_seq, g, w), F32)],
        scratch_shapes=[pltpu.VMEM((g, w), F32), pltpu.VMEM((g, w), F32)],
        compiler_params=_params("parallel", "arbitrary"),
        name="ssm_chunk_scan",
    )(e, h0, a)


def _chunk_out_kernel(u_ref, hin_ref, wt_ref, cst_ref, y_ref):
    for gi in range(SSM_GROUP_TILE):
        intra = jnp.dot(u_ref[gi], wt_ref[gi], preferred_element_type=F32)
        inter = lax.dot_general(hin_ref[gi].astype(BF16), cst_ref[gi], _NT, preferred_element_type=F32)
        y_ref[gi] = intra + inter


def chunk_outputs(u, hin, wt, cst):
    g, nc, w = u.shape
    gt = SSM_GROUP_TILE
    blk = lambda r, c: pl.BlockSpec((gt, r, c), lambda i: (i, 0, 0))
    return pl.pallas_call(
        _chunk_out_kernel,
        grid=(g // gt,),
        in_specs=[blk(nc, w), blk(nc, 2 * STATE_DIM), blk(w, w), blk(w, 2 * STATE_DIM)],
        out_specs=blk(nc, w),
        out_shape=jax.ShapeDtypeStruct((g, nc, w), F32),
        compiler_params=_params("parallel"),
        name="ssm_chunk_outputs",
    )(u, hin, wt, cst)


def s5_prompt(u_bf, n_seq, wt, bst, cst, lb):
    m, d = u_bf.shape
    g, gs, p = N_GROUPS, GROUP_SIZE, STATE_DIM
    nc = m // SSM_CHUNK
    u = u_bf.reshape(nc, SSM_CHUNK, g, gs).transpose(2, 0, 1, 3).reshape(g, nc, CHUNK_W)
    e = chunk_states(u, bst).transpose(1, 0, 2)
    lt_re, lt_im = lb[:, 0, :p], lb[:, 0, p:]
    a = jnp.stack([jnp.concatenate([lt_re, lt_re], axis=1),
                   jnp.concatenate([-lt_im, lt_im], axis=1),
                   jnp.concatenate([lt_im, -lt_im], axis=1)])
    h0 = jnp.zeros((n_seq, g, 2 * p), F32)
    hin, hfin = chunk_scan(e, h0, a)
    y = chunk_outputs(u, hin.transpose(1, 0, 2), wt, cst)
    y = y.reshape(g, nc, SSM_CHUNK, gs).transpose(1, 2, 0, 3).reshape(m, d)
    return y, hfin[:, :, :p], hfin[:, :, p:]


def _ssm_step_kernel(u_ref, hre_ref, him_ref, m0_ref, bre_ref, bim_ref, cre_ref, cim_ref,
                     lre_ref, lim_ref, y_ref, ore_ref, oim_ref):
    u = u_ref[...]
    h_re, h_im = hre_ref[...], him_ref[...]
    l_re, l_im = lre_ref[...], lim_ref[...]
    bmm = lambda a, b: jnp.einsum("gbk,gkn->gbn", a, b, preferred_element_type=F32)
    bmm_nt = lambda a, b: jnp.einsum("gbk,gnk->gbn", a, b, preferred_element_type=F32)
    ore_ref[...] = l_re * h_re - l_im * h_im + bmm(u, bre_ref[...].astype(F32))
    oim_ref[...] = l_re * h_im + l_im * h_re + bmm(u, bim_ref[...].astype(F32))
    y_ref[...] = (bmm(u, m0_ref[...].astype(F32)) + bmm_nt(h_re, cre_ref[...].astype(F32))
                  + bmm_nt(h_im, cim_ref[...].astype(F32)))


def s5_step(u, h_re, h_im, wt, bst, cst, lb):
    b, d = u.shape
    g, gs, p = N_GROUPS, GROUP_SIZE, STATE_DIM
    last = (SSM_CHUNK - 1) * gs
    y, o_re, o_im = pl.pallas_call(
        _ssm_step_kernel,
        out_shape=[jax.ShapeDtypeStruct((g, b, gs), F32),
                   jax.ShapeDtypeStruct((g, b, p), F32), jax.ShapeDtypeStruct((g, b, p), F32)],
        compiler_params=pltpu.CompilerParams(vmem_limit_bytes=VMEM_LIMIT),
        name="ssm_step",
    )(u.reshape(b, g, gs).transpose(1, 0, 2), h_re.transpose(1, 0, 2), h_im.transpose(1, 0, 2),
      wt[:, :gs, :gs], bst[:, last:, :p], bst[:, last:, p:], cst[:, :gs, :p], cst[:, :gs, p:],
      lb[:, 1:2, :p], lb[:, 1:2, p:])
    return y.transpose(1, 0, 2).reshape(b, d), o_re.transpose(1, 0, 2), o_im.transpose(1, 0, 2)


def _select_top_blocks(gate, blk, n_invalid):
    bias = jnp.full(gate.shape, MASK_VALUE, F32)
    for _ in range(TOP_K):
        m = jnp.max(gate, axis=1, keepdims=True)
        first = jnp.min(jnp.where(gate == m, blk, n_invalid), axis=1, keepdims=True)
        hit = blk == first
        bias = jnp.where(hit, jnp.where(m > -jnp.inf, 0.0, bias), bias)
        gate = jnp.where(hit, -jnp.inf, gate)
    return bias


def _query_select_kernel(x_ref, g_ref, wq_ref, ks_ref, o_ref):
    own = pl.program_id(1)
    h = _rms(x_ref[...], g_ref[...]).astype(BF16)
    q = jnp.dot(h, wq_ref[...], preferred_element_type=F32)
    qb = q.astype(BF16)
    means = (ks_ref[0] * (1.0 / MOBA_BLOCK)).astype(BF16)
    blk = lax.broadcasted_iota(jnp.int32, (q.shape[0], LANES), 1)
    past = blk < own
    blk = blk.astype(F32)
    for hh in range(N_HEADS):
        cols = slice(hh * HEAD_DIM, (hh + 1) * HEAD_DIM)
        gate = lax.dot_general(qb[:, cols], means[:, cols], _NT, preferred_element_type=F32)
        gate = jnp.where(past, gate, -jnp.inf)
        o_ref[0, hh, :, :HEAD_DIM] = (q[:, cols] * (ATTN_SCALE * LOG2_E)).astype(BF16)
        o_ref[0, hh, :, HEAD_DIM:] = _select_top_blocks(gate, blk, float(LANES)).astype(BF16)


def query_select(x, g, wq, block_sums, n_seq):
    m, d = x.shape
    seq = m // n_seq
    nb = seq // MOBA_BLOCK
    assert seq % MOBA_BLOCK == 0 and nb <= LANES
    sums = jnp.pad(block_sums.reshape(n_seq, nb, d), ((0, 0), (0, LANES - nb), (0, 0)))
    return pl.pallas_call(
        _query_select_kernel,
        grid=(n_seq, nb),
        in_specs=[pl.BlockSpec((MOBA_BLOCK, d), lambda b, i: (b * nb + i, 0)),
                  _resident((1, d)), _resident((d, d)),
                  pl.BlockSpec((1, LANES, d), lambda b, i: (b, 0, 0))],
        out_specs=pl.BlockSpec((1, N_HEADS, MOBA_BLOCK, 2 * HEAD_DIM), lambda b, i: (b, 0, i, 0)),
        out_shape=jax.ShapeDtypeStruct((n_seq, N_HEADS, seq, 2 * HEAD_DIM), BF16),
        compiler_params=_params("parallel", "parallel"),
        name="moba_query_select",
    )(x, g.reshape(1, d), wq, sums)


def _moba_kernel(q_ref, k_ref, v_ref, o_ref):
    i = pl.program_id(2)
    tq = MOBA_BLOCK
    n_heads = q_ref.shape[1]
    cols = [slice(h * HEAD_DIM, (h + 1) * HEAD_DIM) for h in range(n_heads)]
    own = pl.multiple_of(i * tq, tq)
    row = lax.broadcasted_iota(jnp.int32, (tq, tq), 0)
    col = lax.broadcasted_iota(jnp.int32, (tq, tq), 1)
    causal = col <= row
    state = []
    for h in range(n_heads):
        s = lax.dot_general(q_ref[0, h, :, :HEAD_DIM], k_ref[0, pl.ds(own, tq), cols[h]], _NT,
                            preferred_element_type=F32)
        s = jnp.where(causal, s, MASK_VALUE)
        m = jnp.max(s, axis=1, keepdims=True)
        p = jnp.exp2(s - m)
        l = jnp.sum(p, axis=1, keepdims=True)
        acc = jnp.dot(p.astype(BF16), v_ref[0, pl.ds(own, tq), cols[h]], preferred_element_type=F32)
        state += [m, l, acc]
    lane = lax.broadcasted_iota(jnp.int32, (tq, HEAD_DIM), 1)

    def body(n, carry):
        off = pl.multiple_of(n * tq, tq)
        one_hot = jnp.where(lane == n, 1.0, 0.0).astype(BF16)
        out = []
        for h in range(n_heads):
            m, l, acc = carry[3 * h:3 * h + 3]
            k_ext = jnp.concatenate([k_ref[0, pl.ds(off, tq), cols[h]], one_hot], axis=1)
            s = lax.dot_general(q_ref[0, h], k_ext, _NT, preferred_element_type=F32)
            m_new = jnp.maximum(m, jnp.max(s, axis=1, keepdims=True))
            alpha = jnp.exp2(m - m_new)
            p = jnp.exp2(s - m_new)
            l = alpha * l + jnp.sum(p, axis=1, keepdims=True)
            acc = alpha * acc + jnp.dot(p.astype(BF16), v_ref[0, pl.ds(off, tq), cols[h]],
                                        preferred_element_type=F32)
            out += [m_new, l, acc]
        return tuple(out)

    state = lax.fori_loop(0, i, body, tuple(state))
    for h in range(n_heads):
        _, l, acc = state[3 * h:3 * h + 3]
        o_ref[0, :, cols[h]] = (acc / l).astype(o_ref.dtype)


def moba_prompt(q_ext, k_bf, v_bf):
    n_seq, _, seq, _ = q_ext.shape
    d = k_bf.shape[-1]
    nb = seq // MOBA_BLOCK
    hb = MOBA_HEADS_PER_STEP
    kv = pl.BlockSpec((1, seq, hb * HEAD_DIM), lambda b, h, i: (b, 0, h), pipeline_mode=pl.Buffered(1))
    out = pl.pallas_call(
        _moba_kernel,
        grid=(n_seq, N_HEADS // hb, nb),
        in_specs=[pl.BlockSpec((1, hb, MOBA_BLOCK, 2 * HEAD_DIM), lambda b, h, i: (b, h, i, 0)), kv, kv],
        out_specs=pl.BlockSpec((1, MOBA_BLOCK, hb * HEAD_DIM), lambda b, h, i: (b, i, h)),
        out_shape=jax.ShapeDtypeStruct((n_seq, seq, d), BF16),
        compiler_params=_params("parallel", "parallel", "arbitrary"),
        name="moba_prompt",
    )(q_ext, k_bf, v_bf)
    return out.reshape(n_seq * seq, d)


def _page_sum_kernel(pt_ref, *refs):
    pages, o_ref = refs[:PAGES_PER_STEP], refs[PAGES_PER_STEP]
    per_block = MOBA_BLOCK // PAGE_SIZE
    for r in range(PAGES_PER_STEP // per_block):
        total = jnp.sum(pages[r * per_block][0], axis=0)
        for j in range(1, per_block):
            total = total + jnp.sum(pages[r * per_block + j][0], axis=0)
        o_ref[0, r] = total


def block_sums_paged(cache_k, page_table):
    n_pool, ps, nh, hd = cache_k.shape
    b, n_pages = page_table.shape
    per_block = MOBA_BLOCK // ps
    steps = n_pages // PAGES_PER_STEP
    assert ps == PAGE_SIZE and n_pages % PAGES_PER_STEP == 0
    page = lambda r: pl.BlockSpec((1, ps, nh, hd),
                                  lambda s, t, pt: (pt[s, t * PAGES_PER_STEP + r], 0, 0, 0))
    rows = PAGES_PER_STEP // per_block
    return pl.pallas_call(
        _page_sum_kernel,
        grid_spec=pltpu.PrefetchScalarGridSpec(
            num_scalar_prefetch=1,
            grid=(b, steps),
            in_specs=[page(r) for r in range(PAGES_PER_STEP)],
            out_specs=pl.BlockSpec((1, rows, nh, hd), lambda s, t, pt: (s, t, 0, 0)),
        ),
        out_shape=jax.ShapeDtypeStruct((b, n_pages // per_block, nh, hd), F32),
        compiler_params=_params("parallel", "parallel"),
        name="cache_block_sums",
    )(page_table, *([cache_k] * PAGES_PER_STEP))


def _sample_select_kernel(q_ref, sums_ref, o_ref):
    prod = q_ref[0] * (sums_ref[0] * (1.0 / MOBA_BLOCK))
    gate = jnp.sum(prod, axis=2, keepdims=True)
    nb = gate.shape[0]
    blk = lax.broadcasted_iota(jnp.int32, gate.shape, 0).astype(F32)
    for j in range(TOP_K):
        m = jnp.max(gate, axis=0, keepdims=True)
        first = jnp.min(jnp.where(gate == m, blk, float(nb - 1)), axis=0, keepdims=True)
        o_ref[0, j] = jnp.broadcast_to(first[0], o_ref.shape[2:]).astype(jnp.int32)
        gate = jnp.where(blk == first, -jnp.inf, gate)


def sample_select(q, sums):
    b, nh, hd = q.shape
    nb = sums.shape[1]
    assert nb >= TOP_K
    out = pl.pallas_call(
        _sample_select_kernel,
        grid=(b,),
        in_specs=[pl.BlockSpec((1, nh, hd), lambda s: (s, 0, 0)),
                  pl.BlockSpec((1, nb, nh, hd), lambda s: (s, 0, 0, 0))],
        out_specs=pl.BlockSpec((1, TOP_K, nh, hd), lambda s: (s, 0, 0, 0)),
        out_shape=jax.ShapeDtypeStruct((b, TOP_K, nh, hd), jnp.int32),
        compiler_params=_params("parallel"),
        name="sample_select",
    )(q, sums)
    return out[:, :, :, 0].transpose(0, 2, 1)


def _sample_attn_kernel(n_pages, sel_ref, pt_ref, q_ref, kn_ref, vn_ref, ck_ref, cv_ref, o_ref,
                        kbuf, vbuf, sem):
    step = pl.program_id(0)
    per_block = MOBA_BLOCK // PAGE_SIZE

    def copies(seq, slot):
        out = []
        for h in range(N_HEADS):
            for j in range(n_pages):
                page = pt_ref[seq, sel_ref[seq, h, j // per_block] * per_block + j % per_block]
                out.append(pltpu.make_async_copy(ck_ref.at[page, :, h, :], kbuf.at[slot, h, j], sem.at[slot]))
                out.append(pltpu.make_async_copy(cv_ref.at[page, :, h, :], vbuf.at[slot, h, j], sem.at[slot]))
        return out

    slot = step % 2

    @pl.when(step == 0)
    def _():
        for c in copies(0, 0):
            c.start()

    @pl.when(step + 1 < pl.num_programs(0))
    def _():
        for c in copies(step + 1, 1 - slot):
            c.start()

    for c in copies(step, slot):
        c.wait()

    q = q_ref[0] * ATTN_SCALE
    rows = []
    for h in range(N_HEADS):
        qh = q[h:h + 1]
        s_new = jnp.sum(qh * kn_ref[0, h:h + 1], axis=1, keepdims=True)
        scores = [jnp.sum(kbuf[slot, h, j] * qh, axis=1, keepdims=True) for j in range(n_pages)]
        m = s_new
        for s in scores:
            m = jnp.maximum(m, jnp.max(s, axis=0, keepdims=True))
        p_new = jnp.exp(s_new - m)
        l = p_new
        acc = p_new * vn_ref[0, h:h + 1]
        for j, s in enumerate(scores):
            p = jnp.exp(s - m)
            l = l + jnp.sum(p, axis=0, keepdims=True)
            acc = acc + jnp.sum(p * vbuf[slot, h, j], axis=0, keepdims=True)
        rows.append(acc / l)
    o_ref[0] = jnp.concatenate(rows, axis=0)


def moba_sample(q, k_new, v_new, cache_k, cache_v, page_table, sel):
    b, nh, hd = q.shape
    ps = cache_k.shape[1]
    n_pages = TOP_K * (MOBA_BLOCK // ps)
    seq = pl.BlockSpec((1, nh, hd), lambda s, sel_ref, pt_ref: (s, 0, 0))
    hbm = pl.BlockSpec(memory_space=pl.ANY)
    return pl.pallas_call(
        functools.partial(_sample_attn_kernel, n_pages),
        grid_spec=pltpu.PrefetchScalarGridSpec(
            num_scalar_prefetch=2,
            grid=(b,),
            in_specs=[seq, seq, seq, hbm, hbm],
            out_specs=seq,
            scratch_shapes=[pltpu.VMEM((2, nh, n_pages, ps, hd), F32),
                            pltpu.VMEM((2, nh, n_pages, ps, hd), F32),
                            pltpu.SemaphoreType.DMA((2,))],
        ),
        out_shape=jax.ShapeDtypeStruct((b, nh, hd), F32),
        compiler_params=_params("arbitrary"),
        name="moba_sample",
    )(sel, page_table, q, k_new, v_new, cache_k, cache_v)


def kernel(x_prompt, x_sample, state_ssm_re, state_ssm_im, cache_k, cache_v, page_table,
           norm_ffn1, w_ffn1_gate, w_ffn1_up, w_ffn1_down, norm_mix,
           norm_ffn2, w_ffn2_gate, w_ffn2_up, w_ffn2_down,
           ssm_a_re, ssm_a_im, ssm_log_step, ssm_b_re, ssm_b_im, ssm_c_re, ssm_c_im,
           ssm_d, ssm_w_glu_a, ssm_w_glu_b,
           norm_kv, w_k, w_v, attn_w_q, attn_w_o, norm_final):
    n_seq, seq, d = x_prompt.shape
    n_dec, dec_seq, _ = x_sample.shape
    assert dec_seq == 1 and norm_ffn1.shape[0] == 2 and ssm_a_re.shape[0] == 1
    n_pool, page_size, n_heads, head_dim = cache_k.shape
    assert (page_table.shape[1] * page_size) % MOBA_BLOCK == 0

    bf = lambda w: w.astype(BF16)
    ffn1 = [(norm_ffn1[l], bf(w_ffn1_gate[l]), bf(w_ffn1_up[l]), bf(w_ffn1_down[l])) for l in range(2)]
    ffn2 = [(norm_ffn2[l], bf(w_ffn2_gate[l]), bf(w_ffn2_up[l]), bf(w_ffn2_down[l])) for l in range(2)]
    glu_a, glu_b = bf(ssm_w_glu_a[0]), bf(ssm_w_glu_b[0])
    wk, wv, wq, wo = bf(w_k), bf(w_v), bf(attn_w_q[0]), bf(attn_w_o[0])
    ssm_ops = ssm_weights(ssm_a_re[0], ssm_a_im[0], ssm_log_step[0], ssm_b_re[0], ssm_b_im[0],
                          ssm_c_re[0], ssm_c_im[0])

    def layer_a(x, mixer):
        x = half_ffn(x, *ffn1[0])
        y, h_re, h_im = mixer(x)
        x = ssm_glu(y, x, norm_mix[0], ssm_d[0], glu_a, glu_b)
        return half_ffn(x, *ffn2[0]), h_re, h_im

    xp = x_prompt.reshape(n_seq * seq, d)
    xp, p_re, p_im = layer_a(
        xp, lambda x: s5_prompt(rmsnorm(x, norm_mix[0], BF16), n_seq, *ssm_ops))
    k_p, v_p, kb_p, vb_p, ksum_p = shared_kv(xp, norm_kv, wk, wv, True)
    xp = half_ffn(xp, *ffn1[1])
    q_ext = query_select(xp, norm_mix[1], wq, ksum_p, n_seq)
    attn_p = moba_prompt(q_ext, kb_p.reshape(n_seq, seq, d), vb_p.reshape(n_seq, seq, d))
    xp = proj_residual(attn_p, wo, xp)
    xp = half_ffn(xp, *ffn2[1])
    y_prompt = rmsnorm(xp, norm_final, F32).reshape(n_seq, seq, d)

    xs = x_sample.reshape(n_dec, d)
    xs, s_re, s_im = layer_a(
        xs, lambda x: s5_step(rmsnorm(x, norm_mix[0], F32), state_ssm_re[0], state_ssm_im[0], *ssm_ops))
    k_s, v_s, _, _ = shared_kv(xs, norm_kv, wk, wv, False)
    xs = half_ffn(xs, *ffn1[1])
    q_s = norm_proj(xs, norm_mix[1], wq)
    per_head = lambda a: a.reshape(n_dec, n_heads, head_dim)
    sel = sample_select(per_head(q_s), block_sums_paged(cache_k, page_table))
    attn_s = moba_sample(per_head(q_s), per_head(k_s), per_head(v_s), cache_k, cache_v, page_table, sel)
    xs = proj_residual(attn_s.reshape(n_dec, d).astype(BF16), wo, xs)
    xs = half_ffn(xs, *ffn2[1])
    y_sample = rmsnorm(xs, norm_final, F32).reshape(n_dec, 1, d)

    heads = lambda a, b, s: a.reshape(b, s, n_heads, head_dim)
    return (y_prompt, y_sample, p_re[None], p_im[None], s_re[None], s_im[None],
            heads(k_p, n_seq, seq), heads(v_p, n_seq, seq), heads(k_s, n_dec, 1), heads(v_s, n_dec, 1))
```

```python
import functools

import jax
import jax.numpy as jnp
from jax import lax
from jax.experimental import pallas as pl
from jax.experimental.pallas import tpu as pltpu

F32 = jnp.float32
BF16 = jnp.bfloat16

D_MODEL = 1024
N_GROUPS = 64
GROUP_SIZE = 16
STATE_DIM = 64
N_HEADS = 8
HEAD_DIM = 128
MOBA_BLOCK = 256
TOP_K = 3
PAGE_SIZE = 128
RMS_EPS = 1e-6
ATTN_SCALE = HEAD_DIM ** -0.5
LOG2_E = 1.4426950408889634
MASK_VALUE = -1e30

LANES = 128
SSM_CHUNK = 8
GROUPS_PER_BLOCK = LANES // GROUP_SIZE
BLOCK_STATE = GROUPS_PER_BLOCK * 2 * STATE_DIM
ROW_TILE = 512
FFN_PAGES_ROW_TILE = 256
SSM_CHUNK_TILE = 256
SCAN_TILE = 128
MOBA_HEADS_PER_STEP = 8
VMEM_LIMIT = 56 * 1024 * 1024

_NT = (((1,), (1,)), ((), ()))


def _params(*sem):
    return pltpu.CompilerParams(dimension_semantics=sem, vmem_limit_bytes=VMEM_LIMIT)


def _resident(shape):
    zeros = (0,) * len(shape)
    return pl.BlockSpec(shape, lambda *_: zeros, pipeline_mode=pl.Buffered(1))


def _rms(x, g):
    inv = lax.rsqrt(jnp.mean(x * x, axis=-1, keepdims=True) + RMS_EPS)
    return x * inv * g


def _row_tile(m):
    return ROW_TILE if m % ROW_TILE == 0 else m


def _ffn_kernel(x_ref, g_ref, wg_ref, wu_ref, wd_ref, o_ref):
    x = x_ref[...]
    h = _rms(x, g_ref[...]).astype(BF16)
    a = jnp.dot(h, wg_ref[...], preferred_element_type=F32)
    b = jnp.dot(h, wu_ref[...], preferred_element_type=F32)
    act = (jax.nn.silu(a) * b).astype(BF16)
    o_ref[...] = x + 0.5 * jnp.dot(act, wd_ref[...], preferred_element_type=F32)


def half_ffn(x, g, wg, wu, wd):
    m, d = x.shape
    f = wg.shape[1]
    tm = _row_tile(m)
    row = pl.BlockSpec((tm, d), lambda i: (i, 0))
    return pl.pallas_call(
        _ffn_kernel,
        grid=(m // tm,),
        in_specs=[row, _resident((1, d)), _resident((d, f)), _resident((d, f)), _resident((f, d))],
        out_specs=row,
        out_shape=jax.ShapeDtypeStruct((m, d), F32),
        compiler_params=_params("parallel"),
        name="half_ffn",
    )(x, g.reshape(1, d), wg, wu, wd)


def _ffn_pages_kernel(n_pages, pt_ref, x_ref, g_ref, wg_ref, wu_ref, wd_ref, *refs):
    pages, o_ref, sums_ref = refs[:n_pages], refs[n_pages], refs[n_pages + 1]
    per_block = MOBA_BLOCK // PAGE_SIZE
    for r in range(n_pages // per_block):
        total = jnp.sum(pages[r * per_block][0], axis=0)
        for j in range(1, per_block):
            total = total + jnp.sum(pages[r * per_block + j][0], axis=0)
        sums_ref[0, r] = total
    _ffn_kernel(x_ref, g_ref, wg_ref, wu_ref, wd_ref, o_ref)


def half_ffn_with_page_sums(x, g, wg, wu, wd, cache_k, page_table, first_seq, n_seqs):
    m, d = x.shape
    f = wg.shape[1]
    tm = FFN_PAGES_ROW_TILE if m % FFN_PAGES_ROW_TILE == 0 else m
    steps = m // tm
    n_pool, ps, nh, hd = cache_k.shape
    n_pages = page_table.shape[1]
    per_block = MOBA_BLOCK // ps
    assert ps == PAGE_SIZE and (n_seqs * n_pages) % steps == 0
    per_step = n_seqs * n_pages // steps
    assert per_step % per_block == 0 and n_pages % per_step == 0
    steps_per_seq = n_pages // per_step
    row = pl.BlockSpec((tm, d), lambda i, pt: (i, 0))
    res = lambda shape: pl.BlockSpec(shape, lambda i, pt: (0,) * len(shape), pipeline_mode=pl.Buffered(1))
    page = lambda r: pl.BlockSpec(
        (1, ps, nh, hd),
        lambda i, pt: (pt[first_seq + i // steps_per_seq, (i % steps_per_seq) * per_step + r], 0, 0, 0))
    return pl.pallas_call(
        functools.partial(_ffn_pages_kernel, per_step),
        grid_spec=pltpu.PrefetchScalarGridSpec(
            num_scalar_prefetch=1,
            grid=(steps,),
            in_specs=[row, res((1, d)), res((d, f)), res((d, f)), res((f, d))]
                     + [page(r) for r in range(per_step)],
            out_specs=[row, pl.BlockSpec((1, per_step // per_block, nh, hd),
                                         lambda i, pt: (i // steps_per_seq, i % steps_per_seq, 0, 0))],
        ),
        out_shape=[jax.ShapeDtypeStruct((m, d), F32),
                   jax.ShapeDtypeStruct((n_seqs, n_pages // per_block, nh, hd), F32)],
        compiler_params=_params("parallel"),
        name="half_ffn_page_sums",
    )(page_table, x, g.reshape(1, d), wg, wu, wd, *([cache_k] * per_step))


def _rmsnorm_kernel(x_ref, g_ref, o_ref):
    o_ref[...] = _rms(x_ref[...], g_ref[...]).astype(o_ref.dtype)


def rmsnorm(x, g, dtype):
    m, d = x.shape
    tm = _row_tile(m)
    row = pl.BlockSpec((tm, d), lambda i: (i, 0))
    return pl.pallas_call(
        _rmsnorm_kernel,
        grid=(m // tm,),
        in_specs=[row, _resident((1, d))],
        out_specs=row,
        out_shape=jax.ShapeDtypeStruct((m, d), dtype),
        compiler_params=_params("parallel"),
        name="rmsnorm",
    )(x, g.reshape(1, d))


def _glu_kernel(y_ref, x_ref, g_ref, d_ref, wa_ref, wb_ref, o_ref):
    x = x_ref[...]
    u = _rms(x, g_ref[...])
    y = jnp.concatenate([y_ref[v] for v in range(y_ref.shape[0])], axis=1)
    act = jax.nn.gelu(y + d_ref[...] * u).astype(BF16)
    a = jnp.dot(act, wa_ref[...], preferred_element_type=F32)
    b = jnp.dot(act, wb_ref[...], preferred_element_type=F32)
    o_ref[...] = x + a * jax.nn.sigmoid(b)


def ssm_glu(y, x, g_mix, d_skip, wa, wb):
    m, d = x.shape
    tm = _row_tile(m)
    row = pl.BlockSpec((tm, d), lambda i: (i, 0))
    planes = pl.BlockSpec((d // LANES, tm, LANES), lambda i: (0, i, 0))
    return pl.pallas_call(
        _glu_kernel,
        grid=(m // tm,),
        in_specs=[planes, row, _resident((1, d)), _resident((1, d)), _resident((d, d)), _resident((d, d))],
        out_specs=row,
        out_shape=jax.ShapeDtypeStruct((m, d), F32),
        compiler_params=_params("parallel"),
        name="ssm_glu",
    )(y, x, g_mix.reshape(1, d), d_skip.reshape(1, d), wa, wb)


def _kv_kernel(n_sum, x_ref, g_ref, wk_ref, wv_ref, k_ref, v_ref, kb_ref, vb_ref, *sum_ref):
    h = _rms(x_ref[...], g_ref[...]).astype(BF16)
    k = jnp.dot(h, wk_ref[...], preferred_element_type=F32)
    v = jnp.dot(h, wv_ref[...], preferred_element_type=F32)
    k_ref[...] = k
    v_ref[...] = v
    kb_ref[...] = k.astype(BF16)
    vb_ref[...] = v.astype(BF16)
    for r in range(n_sum):
        sum_ref[0][r] = jnp.sum(k[r * MOBA_BLOCK:(r + 1) * MOBA_BLOCK], axis=0, keepdims=True)


def shared_kv(x, g, wk, wv, with_block_sums):
    m, d = x.shape
    tm = _row_tile(m)
    n_sum = tm // MOBA_BLOCK if with_block_sums else 0
    row = pl.BlockSpec((tm, d), lambda i: (i, 0))
    out_specs = [row, row, row, row]
    out_shape = [jax.ShapeDtypeStruct((m, d), F32)] * 2 + [jax.ShapeDtypeStruct((m, d), BF16)] * 2
    if with_block_sums:
        assert tm % MOBA_BLOCK == 0
        out_specs.append(pl.BlockSpec((n_sum, 1, d), lambda i: (i, 0, 0)))
        out_shape.append(jax.ShapeDtypeStruct((m // MOBA_BLOCK, 1, d), F32))
    return pl.pallas_call(
        functools.partial(_kv_kernel, n_sum),
        grid=(m // tm,),
        in_specs=[row, _resident((1, d)), _resident((d, d)), _resident((d, d))],
        out_specs=out_specs,
        out_shape=out_shape,
        compiler_params=_params("parallel"),
        name="shared_kv",
    )(x, g.reshape(1, d), wk, wv)


def _proj_residual_kernel(a_ref, w_ref, x_ref, o_ref):
    o_ref[...] = x_ref[...] + jnp.dot(a_ref[...], w_ref[...], preferred_element_type=F32)


def proj_residual(a, w, x):
    m, d = x.shape
    tm = _row_tile(m)
    row = pl.BlockSpec((tm, d), lambda i: (i, 0))
    return pl.pallas_call(
        _proj_residual_kernel,
        grid=(m // tm,),
        in_specs=[row, _resident((d, d)), row],
        out_specs=row,
        out_shape=jax.ShapeDtypeStruct((m, d), F32),
        compiler_params=_params("parallel"),
        name="proj_residual",
    )(a, w, x)


def _norm_proj_kernel(x_ref, g_ref, w_ref, o_ref):
    h = _rms(x_ref[...], g_ref[...]).astype(BF16)
    o_ref[...] = jnp.dot(h, w_ref[...], preferred_element_type=F32)


def norm_proj(x, g, w):
    m, d = x.shape
    tm = _row_tile(m)
    row = pl.BlockSpec((tm, d), lambda i: (i, 0))
    return pl.pallas_call(
        _norm_proj_kernel,
        grid=(m // tm,),
        in_specs=[row, _resident((1, d)), _resident((d, d))],
        out_specs=row,
        out_shape=jax.ShapeDtypeStruct((m, d), F32),
        compiler_params=_params("parallel"),
        name="norm_proj",
    )(x, g.reshape(1, d), w)


def _ssm_weights_kernel(are_ref, aim_ref, ls_ref, bre_ref, bim_ref, cre_ref, cim_ref,
                        opw_ref, bst_ref, cst_ref, lb_ref, m0_ref, b0_ref, c1_ref):
    p, gs, t_len = STATE_DIM, GROUP_SIZE, SSM_CHUNK
    zeros = lambda r, c: jnp.zeros((r, c), F32)
    for a in range(GROUPS_PER_BLOCK):
        lam_re = jnp.minimum(are_ref[a], -1e-4)
        lam_im = aim_ref[a]
        dt = jnp.exp(ls_ref[a])
        lag = lax.broadcasted_iota(jnp.int32, (t_len + 8, p), 0).astype(F32)
        mag = jnp.exp(lag * (lam_re * dt))
        ang = lag * (lam_im * dt)
        pw_re = mag * jnp.cos(ang)
        pw_im = mag * jnp.sin(ang)
        lb_re = pw_re[1:2]
        lb_im = pw_im[1:2]
        den = lam_re * lam_re + lam_im * lam_im
        nr = lb_re - 1.0
        coef_re = (nr * lam_re + lb_im * lam_im) / den
        coef_im = (lb_im * lam_re - nr * lam_im) / den
        bt_re = bre_ref[a]
        bt_im = bim_ref[a]
        bb_re = coef_re * bt_re - coef_im * bt_im
        bb_im = coef_re * bt_im + coef_im * bt_re
        bb_cat = jnp.concatenate([bb_re, bb_im], axis=1)
        c_re = cre_ref[a]
        c_im = cim_ref[a]

        def c_pow(t):
            re = c_re * pw_re[t:t + 1] - c_im * pw_im[t:t + 1]
            im = c_re * pw_im[t:t + 1] + c_im * pw_re[t:t + 1]
            return jnp.concatenate([re, -im], axis=1)

        def in_block_rows(x):
            parts = ([zeros(a * gs, x.shape[1])] if a else []) + [x]
            rest = LANES - (a + 1) * gs
            return jnp.concatenate(parts + ([zeros(rest, x.shape[1])] if rest else []), axis=0)

        def in_block_cols(x):
            parts = ([zeros(gs, a * 2 * p)] if a else []) + [x]
            rest = BLOCK_STATE - (a + 1) * 2 * p
            return jnp.concatenate(parts + ([zeros(gs, rest)] if rest else []), axis=1)

        cp = [c_pow(t) for t in range(t_len + 1)]
        spread = jnp.concatenate([in_block_rows(cp[t]) for t in range(t_len)], axis=0)
        mt = lax.dot_general(bb_cat, spread, _NT, precision=lax.Precision.HIGHEST,
                             preferred_element_type=F32)
        for s in range(t_len):
            rows = slice(s * LANES + a * gs, s * LANES + (a + 1) * gs)
            shifted = mt if s == 0 else jnp.concatenate(
                [zeros(gs, s * LANES), mt[:, :(t_len - s) * LANES]], axis=1)
            opw_ref[0, rows, :] = shifted.astype(opw_ref.dtype)
            q = t_len - 1 - s
            re = bb_re * pw_re[q:q + 1] - bb_im * pw_im[q:q + 1]
            im = bb_re * pw_im[q:q + 1] + bb_im * pw_re[q:q + 1]
            bst_ref[0, rows, :] = in_block_cols(jnp.concatenate([re, im], axis=1)).astype(bst_ref.dtype)
            cst_ref[0, rows, :] = in_block_cols(cp[s + 1]).astype(cst_ref.dtype)
        lb_ref[a, 0:1, :] = jnp.concatenate([pw_re[t_len:t_len + 1], pw_im[t_len:t_len + 1]], axis=1)
        lb_ref[a, 1:2, :] = jnp.concatenate([lb_re, lb_im], axis=1)
        lb_ref[a, 2:8, :] = zeros(6, 2 * p)
        m0_ref[a] = mt[:, :LANES]
        b0_ref[a] = bb_cat
        c1_ref[a] = cp[1]


def ssm_weights(a_re, a_im, log_step, b_re, b_im, c_re, c_im):
    g, p, gs, gb = N_GROUPS, STATE_DIM, GROUP_SIZE, GROUPS_PER_BLOCK
    tl = SSM_CHUNK * LANES
    grp = lambda r, c: pl.BlockSpec((gb, r, c), lambda i: (i, 0, 0))
    blk = lambda r, c: pl.BlockSpec((1, r, c), lambda i: (i, 0, 0))
    return pl.pallas_call(
        _ssm_weights_kernel,
        grid=(g // gb,),
        in_specs=[grp(1, p), grp(1, p), grp(1, 1), grp(gs, p), grp(gs, p), grp(gs, p), grp(gs, p)],
        out_specs=[blk(tl, tl), blk(tl, BLOCK_STATE), blk(tl, BLOCK_STATE),
                   grp(8, 2 * p), grp(gs, LANES), grp(gs, 2 * p), grp(gs, 2 * p)],
        out_shape=[jax.ShapeDtypeStruct((g // gb, tl, tl), BF16),
                   jax.ShapeDtypeStruct((g // gb, tl, BLOCK_STATE), BF16),
                   jax.ShapeDtypeStruct((g // gb, tl, BLOCK_STATE), BF16),
                   jax.ShapeDtypeStruct((g, 8, 2 * p), F32),
                   jax.ShapeDtypeStruct((g, gs, LANES), F32),
                   jax.ShapeDtypeStruct((g, gs, 2 * p), F32),
                   jax.ShapeDtypeStruct((g, gs, 2 * p), F32)],
        compiler_params=_params("parallel"),
        name="ssm_weights",
    )(a_re.reshape(g, 1, p), a_im.reshape(g, 1, p), log_step.reshape(g, 1, 1),
      b_re.transpose(0, 2, 1), b_im.transpose(0, 2, 1), c_re, c_im)


def _lane_block(u_ref, n_chunks):
    return jnp.concatenate(
        [u_ref[pl.ds(t, n_chunks, stride=SSM_CHUNK), :] for t in range(SSM_CHUNK)], axis=1).astype(BF16)


def _chunk_state_kernel(n_chunks, u_ref, bst_ref, e_ref):
    e_ref[0] = jnp.dot(_lane_block(u_ref, n_chunks), bst_ref[0], preferred_element_type=F32)


def _chunk_tile(nc):
    return SSM_CHUNK_TILE if nc % SSM_CHUNK_TILE == 0 else nc


def chunk_states(u, bst):
    m, d = u.shape
    nv = d // LANES
    nc = m // SSM_CHUNK
    nct = _chunk_tile(nc)
    return pl.pallas_call(
        functools.partial(_chunk_state_kernel, nct),
        grid=(nv, nc // nct),
        in_specs=[pl.BlockSpec((nct * SSM_CHUNK, LANES), lambda v, i: (i, v)),
                  pl.BlockSpec((1,) + bst.shape[1:], lambda v, i: (v, 0, 0))],
        out_specs=pl.BlockSpec((1, nct, BLOCK_STATE), lambda v, i: (v, i, 0)),
        out_shape=jax.ShapeDtypeStruct((nv, nc, BLOCK_STATE), F32),
        compiler_params=_params("parallel", "parallel"),
        name="ssm_chunk_states",
    )(u, bst)


def _swap_halves(x):
    half = STATE_DIM
    n = x.shape[-1]
    lane = lax.broadcasted_iota(jnp.int32, x.shape, x.ndim - 1)
    return jnp.where(lane % (2 * half) < half, pltpu.roll(x, n - half, axis=x.ndim - 1),
                     pltpu.roll(x, half, axis=x.ndim - 1))


def _scan_kernel(n_steps, e_ref, h0_ref, a_ref, hin_ref, hfin_ref, h_scr, hs_scr):
    @pl.when(pl.program_id(1) == 0)
    def _():
        h_scr[...] = h0_ref[0]
        hs_scr[...] = _swap_halves(h0_ref[0])

    a_same = a_ref[0]
    a_cross = a_ref[1]
    a_cross_s = a_ref[2]

    def body(t, carry):
        h, hs = carry
        e = e_ref[t]
        es = _swap_halves(e)
        hin_ref[t] = h.astype(hin_ref.dtype)
        return a_same * h + a_cross * hs + e, a_same * hs + a_cross_s * h + es

    h, hs = lax.fori_loop(0, n_steps, body, (h_scr[...], hs_scr[...]), unroll=8)
    h_scr[...] = h
    hs_scr[...] = hs
    hfin_ref[0] = h


def chunk_scan(e, h0, a):
    nc, nv, w = e.shape
    n_seq = h0.shape[0]
    per_seq = nc // n_seq
    ct = SCAN_TILE if per_seq % SCAN_TILE == 0 else per_seq
    n_ct = per_seq // ct
    tile = pl.BlockSpec((ct, nv, w), lambda b, c: (b * n_ct + c, 0, 0))
    seq = pl.BlockSpec((1, nv, w), lambda b, c: (b, 0, 0))
    return pl.pallas_call(
        functools.partial(_scan_kernel, ct),
        grid=(n_seq, n_ct),
        in_specs=[tile, seq, _resident((3, nv, w))],
        out_specs=[tile, seq],
        out_shape=[jax.ShapeDtypeStruct((nc, nv, w), BF16), jax.ShapeDtypeStruct((n_seq, nv, w), F32)],
        scratch_shapes=[pltpu.VMEM((nv, w), F32), pltpu.VMEM((nv, w), F32)],
        compiler_params=_params("parallel", "arbitrary"),
        name="ssm_chunk_scan",
    )(e, h0, a)


def _chunk_out_kernel(n_chunks, u_ref, hin_ref, opw_ref, cst_ref, y_ref):
    w = _lane_block(u_ref, n_chunks)
    hin = hin_ref[0]
    tile = 2 * LANES
    for c in range(SSM_CHUNK * LANES // tile):
        cols = slice(c * tile, (c + 1) * tile)
        k = (c + 1) * tile
        y = jnp.dot(w[:, :k], opw_ref[0, :k, cols], preferred_element_type=F32)
        y = y + lax.dot_general(hin, cst_ref[0, cols, :], _NT, preferred_element_type=F32)
        for j in range(tile // LANES):
            t = c * (tile // LANES) + j
            y_ref[0, pl.ds(t, n_chunks, stride=SSM_CHUNK), :] = y[:, j * LANES:(j + 1) * LANES]


def chunk_outputs(u, hin, opw, cst):
    m, d = u.shape
    nv, nc, w = hin.shape
    nct = _chunk_tile(nc)
    rows = nct * SSM_CHUNK
    per_block = lambda a: pl.BlockSpec((1,) + a.shape[1:], lambda v, i: (v, 0, 0))
    return pl.pallas_call(
        functools.partial(_chunk_out_kernel, nct),
        grid=(nv, nc // nct),
        in_specs=[pl.BlockSpec((rows, LANES), lambda v, i: (i, v)),
                  pl.BlockSpec((1, nct, w), lambda v, i: (v, i, 0)), per_block(opw), per_block(cst)],
        out_specs=pl.BlockSpec((1, rows, LANES), lambda v, i: (v, i, 0)),
        out_shape=jax.ShapeDtypeStruct((nv, m, LANES), F32),
        compiler_params=_params("parallel", "parallel"),
        name="ssm_chunk_outputs",
    )(u, hin, opw, cst)


def s5_prompt(u, n_seq, opw, bst, cst, lb):
    g, p = N_GROUPS, STATE_DIM
    nv = g // GROUPS_PER_BLOCK
    e = chunk_states(u, bst).transpose(1, 0, 2)
    lt_re, lt_im = lb[:, 0, :p], lb[:, 0, p:]
    per_block = lambda *halves: jnp.concatenate(halves, axis=1).reshape(nv, BLOCK_STATE)
    a = jnp.stack([per_block(lt_re, lt_re), per_block(-lt_im, lt_im), per_block(lt_im, -lt_im)])
    hin, hfin = chunk_scan(e, jnp.zeros((n_seq, nv, BLOCK_STATE), F32), a)
    y = chunk_outputs(u, hin.transpose(1, 0, 2), opw, cst)
    hfin = hfin.reshape(n_seq, g, 2 * p)
    return y, hfin[:, :, :p], hfin[:, :, p:]


def _ssm_step_kernel(u_ref, hre_ref, him_ref, m0_ref, bre_ref, bim_ref, cre_ref, cim_ref,
                     lre_ref, lim_ref, y_ref, ore_ref, oim_ref):
    u = u_ref[...]
    h_re, h_im = hre_ref[...], him_ref[...]
    l_re, l_im = lre_ref[...], lim_ref[...]
    bmm = lambda a, b: jnp.einsum("gbk,gkn->gbn", a, b, preferred_element_type=F32)
    bmm_nt = lambda a, b: jnp.einsum("gbk,gnk->gbn", a, b, preferred_element_type=F32)
    ore_ref[...] = l_re * h_re - l_im * h_im + bmm(u, bre_ref[...])
    oim_ref[...] = l_re * h_im + l_im * h_re + bmm(u, bim_ref[...])
    y_ref[...] = bmm(u, m0_ref[...]) + bmm_nt(h_re, cre_ref[...]) + bmm_nt(h_im, cim_ref[...])


def s5_step(u, h_re, h_im, lb, m0, b0, c1):
    b, d = u.shape
    g, gs, p = N_GROUPS, GROUP_SIZE, STATE_DIM
    m0t = m0.reshape(g, gs, GROUPS_PER_BLOCK, gs).sum(axis=2)
    y, o_re, o_im = pl.pallas_call(
        _ssm_step_kernel,
        out_shape=[jax.ShapeDtypeStruct((g, b, gs), F32),
                   jax.ShapeDtypeStruct((g, b, p), F32), jax.ShapeDtypeStruct((g, b, p), F32)],
        compiler_params=pltpu.CompilerParams(vmem_limit_bytes=VMEM_LIMIT),
        name="ssm_step",
    )(u.reshape(b, g, gs).transpose(1, 0, 2), h_re.transpose(1, 0, 2), h_im.transpose(1, 0, 2),
      m0t, b0[:, :, :p], b0[:, :, p:], c1[:, :, :p], c1[:, :, p:], lb[:, 1:2, :p], lb[:, 1:2, p:])
    y = y.reshape(d // LANES, GROUPS_PER_BLOCK, b, gs).transpose(0, 2, 1, 3).reshape(d // LANES, b, LANES)
    return y, o_re.transpose(1, 0, 2), o_im.transpose(1, 0, 2)


def _select_top_blocks(gate, blk, n_invalid):
    bias = jnp.full(gate.shape, MASK_VALUE, F32)
    for _ in range(TOP_K):
        m = jnp.max(gate, axis=1, keepdims=True)
        first = jnp.min(jnp.where(gate == m, blk, n_invalid), axis=1, keepdims=True)
        hit = blk == first
        bias = jnp.where(hit, jnp.where(m > -jnp.inf, 0.0, bias), bias)
        gate = jnp.where(hit, -jnp.inf, gate)
    return bias


def _query_select_kernel(x_ref, g_ref, wq_ref, ks_ref, o_ref):
    own = pl.program_id(1)
    h = _rms(x_ref[...], g_ref[...]).astype(BF16)
    q = jnp.dot(h, wq_ref[...], preferred_element_type=F32)
    qb = q.astype(BF16)
    means = (ks_ref[0] * (1.0 / MOBA_BLOCK)).astype(BF16)
    blk = lax.broadcasted_iota(jnp.int32, (q.shape[0], LANES), 1)
    past = blk < own
    blk = blk.astype(F32)
    for hh in range(N_HEADS):
        cols = slice(hh * HEAD_DIM, (hh + 1) * HEAD_DIM)
        gate = lax.dot_general(qb[:, cols], means[:, cols], _NT, preferred_element_type=F32)
        gate = jnp.where(past, gate, -jnp.inf)
        o_ref[0, hh, :, :HEAD_DIM] = (q[:, cols] * (ATTN_SCALE * LOG2_E)).astype(BF16)
        o_ref[0, hh, :, HEAD_DIM:] = _select_top_blocks(gate, blk, float(LANES)).astype(BF16)


def query_select(x, g, wq, block_sums, n_seq):
    m, d = x.shape
    seq = m // n_seq
    nb = seq // MOBA_BLOCK
    assert seq % MOBA_BLOCK == 0 and nb <= LANES
    sums = jnp.pad(block_sums.reshape(n_seq, nb, d), ((0, 0), (0, LANES - nb), (0, 0)))
    return pl.pallas_call(
        _query_select_kernel,
        grid=(n_seq, nb),
        in_specs=[pl.BlockSpec((MOBA_BLOCK, d), lambda b, i: (b * nb + i, 0)),
                  _resident((1, d)), _resident((d, d)),
                  pl.BlockSpec((1, LANES, d), lambda b, i: (b, 0, 0))],
        out_specs=pl.BlockSpec((1, N_HEADS, MOBA_BLOCK, 2 * HEAD_DIM), lambda b, i: (b, 0, i, 0)),
        out_shape=jax.ShapeDtypeStruct((n_seq, N_HEADS, seq, 2 * HEAD_DIM), BF16),
        compiler_params=_params("parallel", "parallel"),
        name="moba_query_select",
    )(x, g.reshape(1, d), wq, sums)


def _moba_kernel(q_ref, k_ref, v_ref, o_ref):
    i = pl.program_id(2)
    tq = MOBA_BLOCK
    n_heads = q_ref.shape[1]
    cols = [slice(h * HEAD_DIM, (h + 1) * HEAD_DIM) for h in range(n_heads)]
    own = pl.multiple_of(i * tq, tq)
    row = lax.broadcasted_iota(jnp.int32, (tq, tq), 0)
    col = lax.broadcasted_iota(jnp.int32, (tq, tq), 1)
    causal = col <= row
    ones = jnp.ones((tq, HEAD_DIM), BF16)
    v_ext = lambda off, h: jnp.concatenate([v_ref[0, pl.ds(off, tq), cols[h]], ones], axis=1)
    state = []
    for h in range(n_heads):
        s = lax.dot_general(q_ref[0, h, :, :HEAD_DIM], k_ref[0, pl.ds(own, tq), cols[h]], _NT,
                            preferred_element_type=F32)
        s = jnp.where(causal, s, MASK_VALUE)
        m = jnp.max(s, axis=1, keepdims=True)
        p = jnp.exp2(s - m)
        state += [m, jnp.dot(p.astype(BF16), v_ext(own, h), preferred_element_type=F32)]
    lane = lax.broadcasted_iota(jnp.int32, (tq, HEAD_DIM), 1)

    def body(n, carry):
        off = pl.multiple_of(n * tq, tq)
        one_hot = jnp.where(lane == n, 1.0, 0.0).astype(BF16)
        out = []
        for h in range(n_heads):
            m, acc = carry[2 * h:2 * h + 2]
            k_ext = jnp.concatenate([k_ref[0, pl.ds(off, tq), cols[h]], one_hot], axis=1)
            s = lax.dot_general(q_ref[0, h], k_ext, _NT, preferred_element_type=F32)
            m_new = jnp.maximum(m, jnp.max(s, axis=1, keepdims=True))
            p = jnp.exp2(s - m_new)
            acc = jnp.exp2(m - m_new) * acc + jnp.dot(p.astype(BF16), v_ext(off, h),
                                                      preferred_element_type=F32)
            out += [m_new, acc]
        return tuple(out)

    state = lax.fori_loop(0, i, body, tuple(state))
    for h in range(n_heads):
        acc = state[2 * h + 1]
        o_ref[0, :, cols[h]] = (acc[:, :HEAD_DIM] / acc[:, HEAD_DIM:]).astype(o_ref.dtype)


def moba_prompt(q_ext, k_bf, v_bf):
    n_seq, _, seq, _ = q_ext.shape
    d = k_bf.shape[-1]
    nb = seq // MOBA_BLOCK
    hb = MOBA_HEADS_PER_STEP
    kv = pl.BlockSpec((1, seq, hb * HEAD_DIM), lambda b, h, i: (b, 0, h), pipeline_mode=pl.Buffered(1))
    out = pl.pallas_call(
        _moba_kernel,
        grid=(n_seq, N_HEADS // hb, nb),
        in_specs=[pl.BlockSpec((1, hb, MOBA_BLOCK, 2 * HEAD_DIM), lambda b, h, i: (b, h, i, 0)), kv, kv],
        out_specs=pl.BlockSpec((1, MOBA_BLOCK, hb * HEAD_DIM), lambda b, h, i: (b, i, h)),
        out_shape=jax.ShapeDtypeStruct((n_seq, seq, d), BF16),
        compiler_params=_params("parallel", "parallel", "arbitrary"),
        name="moba_prompt",
    )(q_ext, k_bf, v_bf)
    return out.reshape(n_seq * seq, d)


def _sample_select_kernel(q_ref, sums_ref, o_ref):
    prod = q_ref[0] * (sums_ref[0] * (1.0 / MOBA_BLOCK))
    gate = jnp.sum(prod, axis=2, keepdims=True)
    nb = gate.shape[0]
    blk = lax.broadcasted_iota(jnp.int32, gate.shape, 0).astype(F32)
    for j in range(TOP_K):
        m = jnp.max(gate, axis=0, keepdims=True)
        first = jnp.min(jnp.where(gate == m, blk, float(nb - 1)), axis=0, keepdims=True)
        o_ref[0, j] = jnp.broadcast_to(first[0], o_ref.shape[2:]).astype(jnp.int32)
        gate = jnp.where(blk == first, -jnp.inf, gate)


def sample_select(q, sums):
    b, nh, hd = q.shape
    nb = sums.shape[1]
    assert nb >= TOP_K
    out = pl.pallas_call(
        _sample_select_kernel,
        grid=(b,),
        in_specs=[pl.BlockSpec((1, nh, hd), lambda s: (s, 0, 0)),
                  pl.BlockSpec((1, nb, nh, hd), lambda s: (s, 0, 0, 0))],
        out_specs=pl.BlockSpec((1, TOP_K, nh, hd), lambda s: (s, 0, 0, 0)),
        out_shape=jax.ShapeDtypeStruct((b, TOP_K, nh, hd), jnp.int32),
        compiler_params=_params("parallel"),
        name="sample_select",
    )(q, sums)
    return out[:, :, :, 0].transpose(0, 2, 1)


def _sample_attn_kernel(n_pages, sel_ref, pt_ref, q_ref, kn_ref, vn_ref, ck_ref, cv_ref, o_ref,
                        kbuf, vbuf, sem):
    step = pl.program_id(0)
    per_block = MOBA_BLOCK // PAGE_SIZE

    def copies(seq, slot):
        out = []
        for h in range(N_HEADS):
            for j in range(n_pages):
                page = pt_ref[seq, sel_ref[seq, h, j // per_block] * per_block + j % per_block]
                out.append(pltpu.make_async_copy(ck_ref.at[page, :, h, :], kbuf.at[slot, h, j], sem.at[slot]))
                out.append(pltpu.make_async_copy(cv_ref.at[page, :, h, :], vbuf.at[slot, h, j], sem.at[slot]))
        return out

    slot = step % 2

    @pl.when(step == 0)
    def _():
        for c in copies(0, 0):
            c.start()

    @pl.when(step + 1 < pl.num_programs(0))
    def _():
        for c in copies(step + 1, 1 - slot):
            c.start()

    for c in copies(step, slot):
        c.wait()

    q = q_ref[0] * ATTN_SCALE
    rows = []
    for h in range(N_HEADS):
        qh = q[h:h + 1]
        s_new = jnp.sum(qh * kn_ref[0, h:h + 1], axis=1, keepdims=True)
        scores = [jnp.sum(kbuf[slot, h, j] * qh, axis=1, keepdims=True) for j in range(n_pages)]
        m = s_new
        for s in scores:
            m = jnp.maximum(m, jnp.max(s, axis=0, keepdims=True))
        p_new = jnp.exp(s_new - m)
        l = p_new
        acc = p_new * vn_ref[0, h:h + 1]
        for j, s in enumerate(scores):
            p = jnp.exp(s - m)
            l = l + jnp.sum(p, axis=0, keepdims=True)
            acc = acc + jnp.sum(p * vbuf[slot, h, j], axis=0, keepdims=True)
        rows.append(acc / l)
    o_ref[0] = jnp.concatenate(rows, axis=0)


def moba_sample(q, k_new, v_new, cache_k, cache_v, page_table, sel):
    b, nh, hd = q.shape
    ps = cache_k.shape[1]
    n_pages = TOP_K * (MOBA_BLOCK // ps)
    seq = pl.BlockSpec((1, nh, hd), lambda s, sel_ref, pt_ref: (s, 0, 0))
    hbm = pl.BlockSpec(memory_space=pl.ANY)
    return pl.pallas_call(
        functools.partial(_sample_attn_kernel, n_pages),
        grid_spec=pltpu.PrefetchScalarGridSpec(
            num_scalar_prefetch=2,
            grid=(b,),
            in_specs=[seq, seq, seq, hbm, hbm],
            out_specs=seq,
            scratch_shapes=[pltpu.VMEM((2, nh, n_pages, ps, hd), F32),
                            pltpu.VMEM((2, nh, n_pages, ps, hd), F32),
                            pltpu.SemaphoreType.DMA((2,))],
        ),
        out_shape=jax.ShapeDtypeStruct((b, nh, hd), F32),
        compiler_params=_params("arbitrary"),
        name="moba_sample",
    )(sel, page_table, q, k_new, v_new, cache_k, cache_v)


def kernel(x_prompt, x_sample, state_ssm_re, state_ssm_im, cache_k, cache_v, page_table,
           norm_ffn1, w_ffn1_gate, w_ffn1_up, w_ffn1_down, norm_mix,
           norm_ffn2, w_ffn2_gate, w_ffn2_up, w_ffn2_down,
           ssm_a_re, ssm_a_im, ssm_log_step, ssm_b_re, ssm_b_im, ssm_c_re, ssm_c_im,
           ssm_d, ssm_w_glu_a, ssm_w_glu_b,
           norm_kv, w_k, w_v, attn_w_q, attn_w_o, norm_final):
    n_seq, seq, d = x_prompt.shape
    n_dec, dec_seq, _ = x_sample.shape
    assert dec_seq == 1 and norm_ffn1.shape[0] == 2 and ssm_a_re.shape[0] == 1
    n_pool, page_size, n_heads, head_dim = cache_k.shape
    assert (page_table.shape[1] * page_size) % MOBA_BLOCK == 0

    bf = lambda w: w.astype(BF16)
    ffn1 = [(norm_ffn1[l], bf(w_ffn1_gate[l]), bf(w_ffn1_up[l]), bf(w_ffn1_down[l])) for l in range(2)]
    ffn2 = [(norm_ffn2[l], bf(w_ffn2_gate[l]), bf(w_ffn2_up[l]), bf(w_ffn2_down[l])) for l in range(2)]
    glu_a, glu_b = bf(ssm_w_glu_a[0]), bf(ssm_w_glu_b[0])
    wk, wv, wq, wo = bf(w_k), bf(w_v), bf(attn_w_q[0]), bf(attn_w_o[0])
    opw, bst, cst, lb, m0, b0, c1 = ssm_weights(ssm_a_re[0], ssm_a_im[0], ssm_log_step[0], ssm_b_re[0],
                                                ssm_b_im[0], ssm_c_re[0], ssm_c_im[0])

    n_host = 4
    assert n_dec % n_host == 0
    block_sums = []

    def prompt_ffn(x, w):
        first = len(block_sums) * (n_dec // n_host)
        x, sums = half_ffn_with_page_sums(x, *w, cache_k, page_table, first, n_dec // n_host)
        block_sums.append(sums)
        return x

    def layer_a(x, ffn, mixer):
        x = ffn(x, ffn1[0])
        y, h_re, h_im = mixer(x)
        x = ssm_glu(y, x, norm_mix[0], ssm_d[0], glu_a, glu_b)
        return ffn(x, ffn2[0]), h_re, h_im

    xp = x_prompt.reshape(n_seq * seq, d)
    xp, p_re, p_im = layer_a(
        xp, prompt_ffn, lambda x: s5_prompt(rmsnorm(x, norm_mix[0], F32), n_seq, opw, bst, cst, lb))
    k_p, v_p, kb_p, vb_p, ksum_p = shared_kv(xp, norm_kv, wk, wv, True)
    xp = prompt_ffn(xp, ffn1[1])
    q_ext = query_select(xp, norm_mix[1], wq, ksum_p, n_seq)
    attn_p = moba_prompt(q_ext, kb_p.reshape(n_seq, seq, d), vb_p.reshape(n_seq, seq, d))
    xp = proj_residual(attn_p, wo, xp)
    xp = prompt_ffn(xp, ffn2[1])
    y_prompt = rmsnorm(xp, norm_final, F32).reshape(n_seq, seq, d)

    plain_ffn = lambda x, w: half_ffn(x, *w)
    xs = x_sample.reshape(n_dec, d)
    xs, s_re, s_im = layer_a(
        xs, plain_ffn,
        lambda x: s5_step(rmsnorm(x, norm_mix[0], F32), state_ssm_re[0], state_ssm_im[0], lb, m0, b0, c1))
    k_s, v_s, _, _ = shared_kv(xs, norm_kv, wk, wv, False)
    xs = plain_ffn(xs, ffn1[1])
    q_s = norm_proj(xs, norm_mix[1], wq)
    per_head = lambda a: a.reshape(n_dec, n_heads, head_dim)
    sel = sample_select(per_head(q_s), jnp.concatenate(block_sums, axis=0))
    attn_s = moba_sample(per_head(q_s), per_head(k_s), per_head(v_s), cache_k, cache_v, page_table, sel)
    xs = proj_residual(attn_s.reshape(n_dec, d).astype(BF16), wo, xs)
    xs = plain_ffn(xs, ffn2[1])
    y_sample = rmsnorm(xs, norm_final, F32).reshape(n_dec, 1, d)

    heads = lambda a, b, s: a.reshape(b, s, n_heads, head_dim)
    return (y_prompt, y_sample, p_re[None], p_im[None], s_re[None], s_im[None],
            heads(k_p, n_seq, seq), heads(v_p, n_seq, seq), heads(k_s, n_dec, 1), heads(v_s, n_dec, 1))
```

```python
import functools

import jax
import jax.numpy as jnp
from jax import lax
from jax.experimental import pallas as pl
from jax.experimental.pallas import tpu as pltpu

F32 = jnp.float32
BF16 = jnp.bfloat16

D_MODEL = 1024
N_GROUPS = 64
GROUP_SIZE = 16
STATE_DIM = 64
N_HEADS = 8
HEAD_DIM = 128
MOBA_BLOCK = 256
TOP_K = 3
PAGE_SIZE = 128
RMS_EPS = 1e-6
ATTN_SCALE = HEAD_DIM ** -0.5
LOG2_E = 1.4426950408889634
MASK_VALUE = -1e30

LANES = 128
SSM_CHUNK = 8
GROUPS_PER_BLOCK = LANES // GROUP_SIZE
BLOCK_STATE = GROUPS_PER_BLOCK * 2 * STATE_DIM
ROW_TILE = 512
FFN_PAGES_ROW_TILE = 256
SSM_CHUNK_TILE = 256
SCAN_TILE = 128
MOBA_HEADS_PER_STEP = 8
MOBA_BLOCKS_PER_STEP = 4
VMEM_LIMIT = 56 * 1024 * 1024

_NT = (((1,), (1,)), ((), ()))


def _params(*sem):
    return pltpu.CompilerParams(dimension_semantics=sem, vmem_limit_bytes=VMEM_LIMIT)


def _resident(shape):
    zeros = (0,) * len(shape)
    return pl.BlockSpec(shape, lambda *_: zeros, pipeline_mode=pl.Buffered(1))


def _rms(x, g):
    inv = lax.rsqrt(jnp.mean(x * x, axis=-1, keepdims=True) + RMS_EPS)
    return x * inv * g


def _row_tile(m):
    return ROW_TILE if m % ROW_TILE == 0 else m


def _ffn_value(x_ref, g_ref, wg_ref, wu_ref, wd_ref):
    x = x_ref[...]
    h = _rms(x, g_ref[...]).astype(BF16)
    a = jnp.dot(h, wg_ref[0], preferred_element_type=F32)
    b = jnp.dot(h, wu_ref[0], preferred_element_type=F32)
    act = (jax.nn.silu(a) * b).astype(BF16)
    return x + 0.5 * jnp.dot(act, wd_ref[0], preferred_element_type=F32)


def _ffn_kernel(x_ref, g_ref, wg_ref, wu_ref, wd_ref, o_ref):
    o_ref[...] = _ffn_value(x_ref, g_ref, wg_ref, wu_ref, wd_ref)


def _layer_weights(layer, w):
    return pl.BlockSpec((1,) + w.shape[1:], lambda *_: (layer, 0, 0), pipeline_mode=pl.Buffered(1))


def half_ffn(x, g, wg, wu, wd, layer):
    m, d = x.shape
    tm = _row_tile(m)
    row = pl.BlockSpec((tm, d), lambda i: (i, 0))
    lw = lambda w: _layer_weights(layer, w)
    return pl.pallas_call(
        _ffn_kernel,
        grid=(m // tm,),
        in_specs=[row, _resident((1, d)), lw(wg), lw(wu), lw(wd)],
        out_specs=row,
        out_shape=jax.ShapeDtypeStruct((m, d), F32),
        compiler_params=_params("parallel"),
        name="half_ffn",
    )(x, g[layer].reshape(1, d), wg, wu, wd)


def _ffn_pages_kernel(n_pages, post, pt_ref, x_ref, g_ref, wg_ref, wu_ref, wd_ref, g2_ref, *refs):
    pages, outs = refs[:n_pages], refs[n_pages:]
    sums_ref = outs[-1]
    per_block = MOBA_BLOCK // PAGE_SIZE
    for r in range(n_pages // per_block):
        total = jnp.sum(pages[r * per_block][0], axis=0)
        for j in range(1, per_block):
            total = total + jnp.sum(pages[r * per_block + j][0], axis=0)
        sums_ref[0, r] = total
    x_new = _ffn_value(x_ref, g_ref, wg_ref, wu_ref, wd_ref)
    if post == "normed":
        outs[0][...] = _rms(x_new, g2_ref[...])
        return
    outs[0][...] = x_new
    if post == "x+planes":
        u = _rms(x_new, g2_ref[...])
        for v in range(outs[1].shape[0]):
            outs[1][v] = u[:, v * LANES:(v + 1) * LANES]


def half_ffn_with_page_sums(x, g, wg, wu, wd, layer, cache_k, page_table, first_seq, n_seqs,
                            post="x", g_post=None):
    m, d = x.shape
    tm = FFN_PAGES_ROW_TILE if m % FFN_PAGES_ROW_TILE == 0 else m
    steps = m // tm
    n_pool, ps, nh, hd = cache_k.shape
    n_pages = page_table.shape[1]
    per_block = MOBA_BLOCK // ps
    assert ps == PAGE_SIZE and (n_seqs * n_pages) % steps == 0
    per_step = n_seqs * n_pages // steps
    assert per_step % per_block == 0 and n_pages % per_step == 0
    steps_per_seq = n_pages // per_step
    row = pl.BlockSpec((tm, d), lambda i, pt: (i, 0))
    vec = pl.BlockSpec((1, d), lambda i, pt: (0, 0), pipeline_mode=pl.Buffered(1))
    lw = lambda w: _layer_weights(layer, w)
    page = lambda r: pl.BlockSpec(
        (1, ps, nh, hd),
        lambda i, pt: (pt[first_seq + i // steps_per_seq, (i % steps_per_seq) * per_step + r], 0, 0, 0))
    out_specs, out_shape = [row], [jax.ShapeDtypeStruct((m, d), F32)]
    if post == "x+planes":
        out_specs.append(pl.BlockSpec((d // LANES, tm, LANES), lambda i, pt: (0, i, 0)))
        out_shape.append(jax.ShapeDtypeStruct((d // LANES, m, LANES), F32))
    out_specs.append(pl.BlockSpec((1, per_step // per_block, nh, hd),
                                  lambda i, pt: (i // steps_per_seq, i % steps_per_seq, 0, 0)))
    out_shape.append(jax.ShapeDtypeStruct((n_seqs, n_pages // per_block, nh, hd), F32))
    g_post = g[layer] if g_post is None else g_post
    return pl.pallas_call(
        functools.partial(_ffn_pages_kernel, per_step, post),
        grid_spec=pltpu.PrefetchScalarGridSpec(
            num_scalar_prefetch=1,
            grid=(steps,),
            in_specs=[row, vec, lw(wg), lw(wu), lw(wd), vec] + [page(r) for r in range(per_step)],
            out_specs=out_specs,
        ),
        out_shape=out_shape,
        compiler_params=_params("parallel"),
        name="half_ffn_page_sums",
    )(page_table, x, g[layer].reshape(1, d), wg, wu, wd, g_post.reshape(1, d), *([cache_k] * per_step))


def _rmsnorm_kernel(x_ref, g_ref, o_ref):
    o_ref[...] = _rms(x_ref[...], g_ref[...]).astype(o_ref.dtype)


def rmsnorm(x, g, dtype):
    m, d = x.shape
    tm = _row_tile(m)
    row = pl.BlockSpec((tm, d), lambda i: (i, 0))
    return pl.pallas_call(
        _rmsnorm_kernel,
        grid=(m // tm,),
        in_specs=[row, _resident((1, d))],
        out_specs=row,
        out_shape=jax.ShapeDtypeStruct((m, d), dtype),
        compiler_params=_params("parallel"),
        name="rmsnorm",
    )(x, g.reshape(1, d))


def _glu_kernel(y_ref, x_ref, g_ref, d_ref, wa_ref, wb_ref, o_ref):
    x = x_ref[...]
    u = _rms(x, g_ref[...])
    y = jnp.concatenate([y_ref[v] for v in range(y_ref.shape[0])], axis=1)
    act = jax.nn.gelu(y + d_ref[...] * u).astype(BF16)
    a = jnp.dot(act, wa_ref[...], preferred_element_type=F32)
    b = jnp.dot(act, wb_ref[...], preferred_element_type=F32)
    o_ref[...] = x + a * jax.nn.sigmoid(b)


def ssm_glu(y, x, g_mix, d_skip, wa, wb):
    m, d = x.shape
    tm = _row_tile(m)
    row = pl.BlockSpec((tm, d), lambda i: (i, 0))
    planes = pl.BlockSpec((d // LANES, tm, LANES), lambda i: (0, i, 0))
    return pl.pallas_call(
        _glu_kernel,
        grid=(m // tm,),
        in_specs=[planes, row, _resident((1, d)), _resident((1, d)), _resident((d, d)), _resident((d, d))],
        out_specs=row,
        out_shape=jax.ShapeDtypeStruct((m, d), F32),
        compiler_params=_params("parallel"),
        name="ssm_glu",
    )(y, x, g_mix.reshape(1, d), d_skip.reshape(1, d), wa, wb)


def _kv_kernel(n_sum, x_ref, g_ref, wk_ref, wv_ref, k_ref, v_ref, kb_ref, vb_ref, *sum_ref):
    h = _rms(x_ref[...], g_ref[...]).astype(BF16)
    k = jnp.dot(h, wk_ref[...], preferred_element_type=F32)
    v = jnp.dot(h, wv_ref[...], preferred_element_type=F32)
    k_ref[...] = k
    v_ref[...] = v
    kb_ref[...] = k.astype(BF16)
    vb_ref[...] = v.astype(BF16)
    for r in range(n_sum):
        sum_ref[0][r] = jnp.sum(k[r * MOBA_BLOCK:(r + 1) * MOBA_BLOCK], axis=0, keepdims=True)


def shared_kv(x, g, wk, wv, with_block_sums):
    m, d = x.shape
    tm = _row_tile(m)
    n_sum = tm // MOBA_BLOCK if with_block_sums else 0
    row = pl.BlockSpec((tm, d), lambda i: (i, 0))
    out_specs = [row, row, row, row]
    out_shape = [jax.ShapeDtypeStruct((m, d), F32)] * 2 + [jax.ShapeDtypeStruct((m, d), BF16)] * 2
    if with_block_sums:
        assert tm % MOBA_BLOCK == 0
        out_specs.append(pl.BlockSpec((n_sum, 1, d), lambda i: (i, 0, 0)))
        out_shape.append(jax.ShapeDtypeStruct((m // MOBA_BLOCK, 1, d), F32))
    return pl.pallas_call(
        functools.partial(_kv_kernel, n_sum),
        grid=(m // tm,),
        in_specs=[row, _resident((1, d)), _resident((d, d)), _resident((d, d))],
        out_specs=out_specs,
        out_shape=out_shape,
        compiler_params=_params("parallel"),
        name="shared_kv",
    )(x, g.reshape(1, d), wk, wv)


def _proj_residual_kernel(a_ref, w_ref, x_ref, o_ref):
    o_ref[...] = x_ref[...] + jnp.dot(a_ref[...], w_ref[...], preferred_element_type=F32)


def proj_residual(a, w, x):
    m, d = x.shape
    tm = _row_tile(m)
    row = pl.BlockSpec((tm, d), lambda i: (i, 0))
    return pl.pallas_call(
        _proj_residual_kernel,
        grid=(m // tm,),
        in_specs=[row, _resident((d, d)), row],
        out_specs=row,
        out_shape=jax.ShapeDtypeStruct((m, d), F32),
        compiler_params=_params("parallel"),
        name="proj_residual",
    )(a, w, x)


def _norm_proj_kernel(x_ref, g_ref, w_ref, o_ref):
    h = _rms(x_ref[...], g_ref[...]).astype(BF16)
    o_ref[...] = jnp.dot(h, w_ref[...], preferred_element_type=F32)


def norm_proj(x, g, w):
    m, d = x.shape
    tm = _row_tile(m)
    row = pl.BlockSpec((tm, d), lambda i: (i, 0))
    return pl.pallas_call(
        _norm_proj_kernel,
        grid=(m // tm,),
        in_specs=[row, _resident((1, d)), _resident((d, d))],
        out_specs=row,
        out_shape=jax.ShapeDtypeStruct((m, d), F32),
        compiler_params=_params("parallel"),
        name="norm_proj",
    )(x, g.reshape(1, d), w)


def _ssm_weights_kernel(are_ref, aim_ref, ls_ref, bre_ref, bim_ref, cre_ref, cim_ref,
                        opw_ref, bst_ref, cst_ref, lb_ref, m0_ref, b0_ref, c1_ref):
    p, gs, t_len = STATE_DIM, GROUP_SIZE, SSM_CHUNK
    zeros = lambda r, c: jnp.zeros((r, c), F32)
    for a in range(GROUPS_PER_BLOCK):
        lam_re = jnp.minimum(are_ref[a], -1e-4)
        lam_im = aim_ref[a]
        dt = jnp.exp(ls_ref[a])
        lag = lax.broadcasted_iota(jnp.int32, (t_len + 8, p), 0).astype(F32)
        mag = jnp.exp(lag * (lam_re * dt))
        ang = lag * (lam_im * dt)
        pw_re = mag * jnp.cos(ang)
        pw_im = mag * jnp.sin(ang)
        lb_re = pw_re[1:2]
        lb_im = pw_im[1:2]
        den = lam_re * lam_re + lam_im * lam_im
        nr = lb_re - 1.0
        coef_re = (nr * lam_re + lb_im * lam_im) / den
        coef_im = (lb_im * lam_re - nr * lam_im) / den
        bt_re = bre_ref[a]
        bt_im = bim_ref[a]
        bb_re = coef_re * bt_re - coef_im * bt_im
        bb_im = coef_re * bt_im + coef_im * bt_re
        bb_cat = jnp.concatenate([bb_re, bb_im], axis=1)
        c_re = cre_ref[a]
        c_im = cim_ref[a]

        def c_pow(t):
            re = c_re * pw_re[t:t + 1] - c_im * pw_im[t:t + 1]
            im = c_re * pw_im[t:t + 1] + c_im * pw_re[t:t + 1]
            return jnp.concatenate([re, -im], axis=1)

        def in_block_rows(x):
            parts = ([zeros(a * gs, x.shape[1])] if a else []) + [x]
            rest = LANES - (a + 1) * gs
            return jnp.concatenate(parts + ([zeros(rest, x.shape[1])] if rest else []), axis=0)

        def in_block_cols(x):
            parts = ([zeros(gs, a * 2 * p)] if a else []) + [x]
            rest = BLOCK_STATE - (a + 1) * 2 * p
            return jnp.concatenate(parts + ([zeros(gs, rest)] if rest else []), axis=1)

        cp = [c_pow(t) for t in range(t_len + 1)]
        spread = jnp.concatenate([in_block_rows(cp[t]) for t in range(t_len)], axis=0)
        mt = lax.dot_general(bb_cat, spread, _NT, precision=lax.Precision.HIGHEST,
                             preferred_element_type=F32)
        for s in range(t_len):
            rows = slice(s * LANES + a * gs, s * LANES + (a + 1) * gs)
            shifted = mt if s == 0 else jnp.concatenate(
                [zeros(gs, s * LANES), mt[:, :(t_len - s) * LANES]], axis=1)
            opw_ref[0, rows, :] = shifted.astype(opw_ref.dtype)
            q = t_len - 1 - s
            re = bb_re * pw_re[q:q + 1] - bb_im * pw_im[q:q + 1]
            im = bb_re * pw_im[q:q + 1] + bb_im * pw_re[q:q + 1]
            bst_ref[0, rows, :] = in_block_cols(jnp.concatenate([re, im], axis=1)).astype(bst_ref.dtype)
            cst_ref[0, rows, :] = in_block_cols(cp[s + 1]).astype(cst_ref.dtype)
        lb_ref[a, 0:1, :] = jnp.concatenate([pw_re[t_len:t_len + 1], pw_im[t_len:t_len + 1]], axis=1)
        lb_ref[a, 1:2, :] = jnp.concatenate([lb_re, lb_im], axis=1)
        lb_ref[a, 2:8, :] = zeros(6, 2 * p)
        m0_ref[a] = mt[:, :LANES]
        b0_ref[a] = bb_cat
        c1_ref[a] = cp[1]


def ssm_weights(a_re, a_im, log_step, b_re, b_im, c_re, c_im):
    g, p, gs, gb = N_GROUPS, STATE_DIM, GROUP_SIZE, GROUPS_PER_BLOCK
    tl = SSM_CHUNK * LANES
    grp = lambda r, c: pl.BlockSpec((gb, r, c), lambda i: (i, 0, 0))
    blk = lambda r, c: pl.BlockSpec((1, r, c), lambda i: (i, 0, 0))
    return pl.pallas_call(
        _ssm_weights_kernel,
        grid=(g // gb,),
        in_specs=[grp(1, p), grp(1, p), grp(1, 1), grp(gs, p), grp(gs, p), grp(gs, p), grp(gs, p)],
        out_specs=[blk(tl, tl), blk(tl, BLOCK_STATE), blk(tl, BLOCK_STATE),
                   grp(8, 2 * p), grp(gs, LANES), grp(gs, 2 * p), grp(gs, 2 * p)],
        out_shape=[jax.ShapeDtypeStruct((g // gb, tl, tl), BF16),
                   jax.ShapeDtypeStruct((g // gb, tl, BLOCK_STATE), BF16),
                   jax.ShapeDtypeStruct((g // gb, tl, BLOCK_STATE), BF16),
                   jax.ShapeDtypeStruct((g, 8, 2 * p), F32),
                   jax.ShapeDtypeStruct((g, gs, LANES), F32),
                   jax.ShapeDtypeStruct((g, gs, 2 * p), F32),
                   jax.ShapeDtypeStruct((g, gs, 2 * p), F32)],
        compiler_params=_params("parallel"),
        name="ssm_weights",
    )(a_re.reshape(g, 1, p), a_im.reshape(g, 1, p), log_step.reshape(g, 1, 1),
      b_re.transpose(0, 2, 1), b_im.transpose(0, 2, 1), c_re, c_im)


def _lane_block(u_ref, n_chunks):
    return jnp.concatenate(
        [u_ref[0, pl.ds(t, n_chunks, stride=SSM_CHUNK), :] for t in range(SSM_CHUNK)], axis=1).astype(BF16)


def _chunk_state_kernel(n_chunks, u_ref, bst_ref, e_ref):
    e_ref[0] = jnp.dot(_lane_block(u_ref, n_chunks), bst_ref[0], preferred_element_type=F32)


def _chunk_tile(nc):
    return SSM_CHUNK_TILE if nc % SSM_CHUNK_TILE == 0 else nc


def chunk_states(u, bst):
    nv, m, _ = u.shape
    nc = m // SSM_CHUNK
    nct = _chunk_tile(nc)
    return pl.pallas_call(
        functools.partial(_chunk_state_kernel, nct),
        grid=(nv, nc // nct),
        in_specs=[pl.BlockSpec((1, nct * SSM_CHUNK, LANES), lambda v, i: (v, i, 0)),
                  pl.BlockSpec((1,) + bst.shape[1:], lambda v, i: (v, 0, 0))],
        out_specs=pl.BlockSpec((1, nct, BLOCK_STATE), lambda v, i: (v, i, 0)),
        out_shape=jax.ShapeDtypeStruct((nv, nc, BLOCK_STATE), F32),
        compiler_params=_params("parallel", "parallel"),
        name="ssm_chunk_states",
    )(u, bst)


def _swap_halves(x):
    half = STATE_DIM
    n = x.shape[-1]
    lane = lax.broadcasted_iota(jnp.int32, x.shape, x.ndim - 1)
    return jnp.where(lane % (2 * half) < half, pltpu.roll(x, n - half, axis=x.ndim - 1),
                     pltpu.roll(x, half, axis=x.ndim - 1))


def _scan_kernel(n_steps, e_ref, h0_ref, a_ref, hin_ref, hfin_ref, h_scr, hs_scr):
    @pl.when(pl.program_id(1) == 0)
    def _():
        h_scr[...] = h0_ref[0]
        hs_scr[...] = _swap_halves(h0_ref[0])

    a_same = a_ref[0]
    a_cross = a_ref[1]
    a_cross_s = a_ref[2]

    def body(t, carry):
        h, hs = carry
        e = e_ref[t]
        es = _swap_halves(e)
        hin_ref[t] = h.astype(hin_ref.dtype)
        return a_same * h + a_cross * hs + e, a_same * hs + a_cross_s * h + es

    h, hs = lax.fori_loop(0, n_steps, body, (h_scr[...], hs_scr[...]), unroll=8)
    h_scr[...] = h
    hs_scr[...] = hs
    hfin_ref[0] = h


def chunk_scan(e, h0, a):
    nc, nv, w = e.shape
    n_seq = h0.shape[0]
    per_seq = nc // n_seq
    ct = SCAN_TILE if per_seq % SCAN_TILE == 0 else per_seq
    n_ct = per_seq // ct
    tile = pl.BlockSpec((ct, nv, w), lambda b, c: (b * n_ct + c, 0, 0))
    seq = pl.BlockSpec((1, nv, w), lambda b, c: (b, 0, 0))
    return pl.pallas_call(
        functools.partial(_scan_kernel, ct),
        grid=(n_seq, n_ct),
        in_specs=[tile, seq, _resident((3, nv, w))],
        out_specs=[tile, seq],
        out_shape=[jax.ShapeDtypeStruct((nc, nv, w), BF16), jax.ShapeDtypeStruct((n_seq, nv, w), F32)],
        scratch_shapes=[pltpu.VMEM((nv, w), F32), pltpu.VMEM((nv, w), F32)],
        compiler_params=_params("parallel", "arbitrary"),
        name="ssm_chunk_scan",
    )(e, h0, a)


def _chunk_out_kernel(n_chunks, u_ref, hin_ref, opw_ref, cst_ref, y_ref):
    w = _lane_block(u_ref, n_chunks)
    hin = hin_ref[0]
    tile = 2 * LANES
    for c in range(SSM_CHUNK * LANES // tile):
        cols = slice(c * tile, (c + 1) * tile)
        k = (c + 1) * tile
        y = jnp.dot(w[:, :k], opw_ref[0, :k, cols], preferred_element_type=F32)
        y = y + lax.dot_general(hin, cst_ref[0, cols, :], _NT, preferred_element_type=F32)
        for j in range(tile // LANES):
            t = c * (tile // LANES) + j
            y_ref[0, pl.ds(t, n_chunks, stride=SSM_CHUNK), :] = y[:, j * LANES:(j + 1) * LANES]


def chunk_outputs(u, hin, opw, cst):
    m = u.shape[1]
    nv, nc, w = hin.shape
    nct = _chunk_tile(nc)
    rows = nct * SSM_CHUNK
    per_block = lambda a: pl.BlockSpec((1,) + a.shape[1:], lambda v, i: (v, 0, 0))
    return pl.pallas_call(
        functools.partial(_chunk_out_kernel, nct),
        grid=(nv, nc // nct),
        in_specs=[pl.BlockSpec((1, rows, LANES), lambda v, i: (v, i, 0)),
                  pl.BlockSpec((1, nct, w), lambda v, i: (v, i, 0)), per_block(opw), per_block(cst)],
        out_specs=pl.BlockSpec((1, rows, LANES), lambda v, i: (v, i, 0)),
        out_shape=jax.ShapeDtypeStruct((nv, m, LANES), F32),
        compiler_params=_params("parallel", "parallel"),
        name="ssm_chunk_outputs",
    )(u, hin, opw, cst)


def s5_prompt(u, n_seq, opw, bst, cst, lb):
    g, p = N_GROUPS, STATE_DIM
    nv = g // GROUPS_PER_BLOCK
    e = chunk_states(u, bst).transpose(1, 0, 2)
    lt_re, lt_im = lb[:, 0, :p], lb[:, 0, p:]
    per_block = lambda *halves: jnp.concatenate(halves, axis=1).reshape(nv, BLOCK_STATE)
    a = jnp.stack([per_block(lt_re, lt_re), per_block(-lt_im, lt_im), per_block(lt_im, -lt_im)])
    hin, hfin = chunk_scan(e, jnp.zeros((n_seq, nv, BLOCK_STATE), F32), a)
    y = chunk_outputs(u, hin.transpose(1, 0, 2), opw, cst)
    hfin = hfin.reshape(n_seq, g, 2 * p)
    return y, hfin[:, :, :p], hfin[:, :, p:]


def _ssm_step_kernel(u_ref, hre_ref, him_ref, m0_ref, bre_ref, bim_ref, cre_ref, cim_ref,
                     lre_ref, lim_ref, y_ref, ore_ref, oim_ref):
    u = u_ref[...]
    h_re, h_im = hre_ref[...], him_ref[...]
    l_re, l_im = lre_ref[...], lim_ref[...]
    bmm = lambda a, b: jnp.einsum("gbk,gkn->gbn", a, b, preferred_element_type=F32)
    bmm_nt = lambda a, b: jnp.einsum("gbk,gnk->gbn", a, b, preferred_element_type=F32)
    ore_ref[...] = l_re * h_re - l_im * h_im + bmm(u, bre_ref[...])
    oim_ref[...] = l_re * h_im + l_im * h_re + bmm(u, bim_ref[...])
    y_ref[...] = bmm(u, m0_ref[...]) + bmm_nt(h_re, cre_ref[...]) + bmm_nt(h_im, cim_ref[...])


def s5_step(u, h_re, h_im, lb, m0, b0, c1):
    b, d = u.shape
    g, gs, p = N_GROUPS, GROUP_SIZE, STATE_DIM
    m0t = m0.reshape(g, gs, GROUPS_PER_BLOCK, gs).sum(axis=2)
    y, o_re, o_im = pl.pallas_call(
        _ssm_step_kernel,
        out_shape=[jax.ShapeDtypeStruct((g, b, gs), F32),
                   jax.ShapeDtypeStruct((g, b, p), F32), jax.ShapeDtypeStruct((g, b, p), F32)],
        compiler_params=pltpu.CompilerParams(vmem_limit_bytes=VMEM_LIMIT),
        name="ssm_step",
    )(u.reshape(b, g, gs).transpose(1, 0, 2), h_re.transpose(1, 0, 2), h_im.transpose(1, 0, 2),
      m0t, b0[:, :, :p], b0[:, :, p:], c1[:, :, :p], c1[:, :, p:], lb[:, 1:2, :p], lb[:, 1:2, p:])
    y = y.reshape(d // LANES, GROUPS_PER_BLOCK, b, gs).transpose(0, 2, 1, 3).reshape(d // LANES, b, LANES)
    return y, o_re.transpose(1, 0, 2), o_im.transpose(1, 0, 2)


def _select_top_blocks(gate, blk):
    bias = jnp.full(gate.shape, MASK_VALUE, F32)
    for _ in range(TOP_K):
        m = jnp.max(gate, axis=0, keepdims=True)
        first = jnp.min(jnp.where(gate == m, blk, float(gate.shape[0])), axis=0, keepdims=True)
        hit = blk == first
        bias = jnp.where(hit, jnp.where(m > -jnp.inf, 0.0, bias), bias)
        gate = jnp.where(hit, -jnp.inf, gate)
    return bias


def _query_select_kernel(x_ref, g_ref, wq_ref, ks_ref, o_ref):
    own = pl.program_id(1)
    h = _rms(x_ref[...], g_ref[...]).astype(BF16)
    q = jnp.dot(h, wq_ref[...], preferred_element_type=F32)
    qb = q.astype(BF16)
    means = (ks_ref[0] * (1.0 / MOBA_BLOCK)).astype(BF16)
    nb8, tq = means.shape[0], q.shape[0]
    blk = lax.broadcasted_iota(jnp.int32, (nb8, tq), 0)
    past = blk < own
    blk = blk.astype(F32)
    unused = jnp.full((LANES - nb8, tq), MASK_VALUE, F32)
    for hh in range(N_HEADS):
        cols = slice(hh * HEAD_DIM, (hh + 1) * HEAD_DIM)
        gate = lax.dot_general(means[:, cols], qb[:, cols], _NT, preferred_element_type=F32)
        bias = _select_top_blocks(jnp.where(past, gate, -jnp.inf), blk)
        o_ref[0, hh, :, :HEAD_DIM] = (q[:, cols] * (ATTN_SCALE * LOG2_E)).astype(BF16)
        o_ref[0, hh, :, HEAD_DIM:] = jnp.concatenate([bias, unused], axis=0).T.astype(BF16)


def query_select(x, g, wq, block_sums, n_seq):
    m, d = x.shape
    seq = m // n_seq
    nb = seq // MOBA_BLOCK
    nb8 = -(-nb // 8) * 8
    assert seq % MOBA_BLOCK == 0 and nb8 <= LANES
    sums = jnp.pad(block_sums.reshape(n_seq, nb, d), ((0, 0), (0, nb8 - nb), (0, 0)))
    return pl.pallas_call(
        _query_select_kernel,
        grid=(n_seq, nb),
        in_specs=[pl.BlockSpec((MOBA_BLOCK, d), lambda b, i: (b * nb + i, 0)),
                  _resident((1, d)), _resident((d, d)),
                  pl.BlockSpec((1, nb8, d), lambda b, i: (b, 0, 0))],
        out_specs=pl.BlockSpec((1, N_HEADS, MOBA_BLOCK, 2 * HEAD_DIM), lambda b, i: (b, 0, i, 0)),
        out_shape=jax.ShapeDtypeStruct((n_seq, N_HEADS, seq, 2 * HEAD_DIM), BF16),
        compiler_params=_params("parallel", "parallel"),
        name="moba_query_select",
    )(x, g.reshape(1, d), wq, sums)


def _moba_kernel(q_ref, k_ref, v_ref, o_ref):
    i = pl.program_id(2)
    tq = MOBA_BLOCK
    span = MOBA_BLOCKS_PER_STEP
    n_heads = q_ref.shape[1]
    cols = [slice(h * HEAD_DIM, (h + 1) * HEAD_DIM) for h in range(n_heads)]
    own = pl.multiple_of(i * tq, tq)
    row = lax.broadcasted_iota(jnp.int32, (tq, tq), 0)
    col = lax.broadcasted_iota(jnp.int32, (tq, tq), 1)
    causal = col <= row
    ones = jnp.ones((span * tq, HEAD_DIM), BF16)
    state = []
    for h in range(n_heads):
        s = lax.dot_general(q_ref[0, h, :, :HEAD_DIM], k_ref[0, pl.ds(own, tq), cols[h]], _NT,
                            preferred_element_type=F32)
        s = jnp.where(causal, s, MASK_VALUE)
        m = jnp.max(s, axis=1, keepdims=True)
        p = jnp.exp2(s - m)
        v_own = jnp.concatenate([v_ref[0, pl.ds(own, tq), cols[h]], ones[:tq]], axis=1)
        state += [m, jnp.dot(p.astype(BF16), v_own, preferred_element_type=F32)]
    lane = lax.broadcasted_iota(jnp.int32, (span * tq, HEAD_DIM), 1)
    block = lax.broadcasted_iota(jnp.int32, (span * tq, HEAD_DIM), 0) // tq

    def body(g, carry):
        off = pl.multiple_of(g * (span * tq), span * tq)
        one_hot = jnp.where(lane == g * span + block, 1.0, 0.0).astype(BF16)
        out = []
        for h in range(n_heads):
            m, acc = carry[2 * h:2 * h + 2]
            k_ext = jnp.concatenate([k_ref[0, pl.ds(off, span * tq), cols[h]], one_hot], axis=1)
            s = lax.dot_general(q_ref[0, h], k_ext, _NT, preferred_element_type=F32)
            m_new = jnp.maximum(m, jnp.max(s, axis=1, keepdims=True))
            p = jnp.exp2(s - m_new)
            v_ext = jnp.concatenate([v_ref[0, pl.ds(off, span * tq), cols[h]], ones], axis=1)
            acc = jnp.exp2(m - m_new) * acc + jnp.dot(p.astype(BF16), v_ext, preferred_element_type=F32)
            out += [m_new, acc]
        return tuple(out)

    state = lax.fori_loop(0, (i + span - 1) // span, body, tuple(state))
    for h in range(n_heads):
        acc = state[2 * h + 1]
        o_ref[0, :, cols[h]] = (acc[:, :HEAD_DIM] / acc[:, HEAD_DIM:]).astype(o_ref.dtype)


def moba_prompt(q_ext, k_bf, v_bf):
    n_seq, _, seq, _ = q_ext.shape
    d = k_bf.shape[-1]
    nb = seq // MOBA_BLOCK
    hb = MOBA_HEADS_PER_STEP
    assert seq % (MOBA_BLOCKS_PER_STEP * MOBA_BLOCK) == 0
    kv = pl.BlockSpec((1, seq, hb * HEAD_DIM), lambda b, h, i: (b, 0, h), pipeline_mode=pl.Buffered(1))
    out = pl.pallas_call(
        _moba_kernel,
        grid=(n_seq, N_HEADS // hb, nb),
        in_specs=[pl.BlockSpec((1, hb, MOBA_BLOCK, 2 * HEAD_DIM), lambda b, h, i: (b, h, i, 0)), kv, kv],
        out_specs=pl.BlockSpec((1, MOBA_BLOCK, hb * HEAD_DIM), lambda b, h, i: (b, i, h)),
        out_shape=jax.ShapeDtypeStruct((n_seq, seq, d), BF16),
        compiler_params=_params("parallel", "parallel", "arbitrary"),
        name="moba_prompt",
    )(q_ext, k_bf, v_bf)
    return out.reshape(n_seq * seq, d)


def _sample_select_kernel(q_ref, sums_ref, o_ref):
    prod = q_ref[0] * (sums_ref[0] * (1.0 / MOBA_BLOCK))
    gate = jnp.sum(prod, axis=2, keepdims=True)
    nb = gate.shape[0]
    blk = lax.broadcasted_iota(jnp.int32, gate.shape, 0).astype(F32)
    for j in range(TOP_K):
        m = jnp.max(gate, axis=0, keepdims=True)
        first = jnp.min(jnp.where(gate == m, blk, float(nb - 1)), axis=0, keepdims=True)
        o_ref[0, j] = jnp.broadcast_to(first[0], o_ref.shape[2:]).astype(jnp.int32)
        gate = jnp.where(blk == first, -jnp.inf, gate)


def sample_select(q, sums):
    b, nh, hd = q.shape
    nb = sums.shape[1]
    assert nb >= TOP_K
    out = pl.pallas_call(
        _sample_select_kernel,
        grid=(b,),
        in_specs=[pl.BlockSpec((1, nh, hd), lambda s: (s, 0, 0)),
                  pl.BlockSpec((1, nb, nh, hd), lambda s: (s, 0, 0, 0))],
        out_specs=pl.BlockSpec((1, TOP_K, nh, hd), lambda s: (s, 0, 0, 0)),
        out_shape=jax.ShapeDtypeStruct((b, TOP_K, nh, hd), jnp.int32),
        compiler_params=_params("parallel"),
        name="sample_select",
    )(q, sums)
    return out[:, :, :, 0].transpose(0, 2, 1)


def _sample_attn_kernel(n_pages, sel_ref, pt_ref, q_ref, kn_ref, vn_ref, ck_ref, cv_ref, o_ref,
                        kbuf, vbuf, sem):
    step = pl.program_id(0)
    per_block = MOBA_BLOCK // PAGE_SIZE

    def copies(seq, slot):
        out = []
        for h in range(N_HEADS):
            for j in range(n_pages):
                page = pt_ref[seq, sel_ref[seq, h, j // per_block] * per_block + j % per_block]
                out.append(pltpu.make_async_copy(ck_ref.at[page, :, h, :], kbuf.at[slot, h, j], sem.at[slot]))
                out.append(pltpu.make_async_copy(cv_ref.at[page, :, h, :], vbuf.at[slot, h, j], sem.at[slot]))
        return out

    slot = step % 2

    @pl.when(step == 0)
    def _():
        for c in copies(0, 0):
            c.start()

    @pl.when(step + 1 < pl.num_programs(0))
    def _():
        for c in copies(step + 1, 1 - slot):
            c.start()

    for c in copies(step, slot):
        c.wait()

    q = q_ref[0] * ATTN_SCALE
    rows = []
    for h in range(N_HEADS):
        qh = q[h:h + 1]
        s_new = jnp.sum(qh * kn_ref[0, h:h + 1], axis=1, keepdims=True)
        scores = [jnp.sum(kbuf[slot, h, j] * qh, axis=1, keepdims=True) for j in range(n_pages)]
        m = s_new
        for s in scores:
            m = jnp.maximum(m, jnp.max(s, axis=0, keepdims=True))
        p_new = jnp.exp(s_new - m)
        l = p_new
        acc = p_new * vn_ref[0, h:h + 1]
        for j, s in enumerate(scores):
            p = jnp.exp(s - m)
            l = l + jnp.sum(p, axis=0, keepdims=True)
            acc = acc + jnp.sum(p * vbuf[slot, h, j], axis=0, keepdims=True)
        rows.append(acc / l)
    o_ref[0] = jnp.concatenate(rows, axis=0)


def moba_sample(q, k_new, v_new, cache_k, cache_v, page_table, sel):
    b, nh, hd = q.shape
    ps = cache_k.shape[1]
    n_pages = TOP_K * (MOBA_BLOCK // ps)
    seq = pl.BlockSpec((1, nh, hd), lambda s, sel_ref, pt_ref: (s, 0, 0))
    hbm = pl.BlockSpec(memory_space=pl.ANY)
    return pl.pallas_call(
        functools.partial(_sample_attn_kernel, n_pages),
        grid_spec=pltpu.PrefetchScalarGridSpec(
            num_scalar_prefetch=2,
            grid=(b,),
            in_specs=[seq, seq, seq, hbm, hbm],
            out_specs=seq,
            scratch_shapes=[pltpu.VMEM((2, nh, n_pages, ps, hd), F32),
                            pltpu.VMEM((2, nh, n_pages, ps, hd), F32),
                            pltpu.SemaphoreType.DMA((2,))],
        ),
        out_shape=jax.ShapeDtypeStruct((b, nh, hd), F32),
        compiler_params=_params("arbitrary"),
        name="moba_sample",
    )(sel, page_table, q, k_new, v_new, cache_k, cache_v)


def kernel(x_prompt, x_sample, state_ssm_re, state_ssm_im, cache_k, cache_v, page_table,
           norm_ffn1, w_ffn1_gate, w_ffn1_up, w_ffn1_down, norm_mix,
           norm_ffn2, w_ffn2_gate, w_ffn2_up, w_ffn2_down,
           ssm_a_re, ssm_a_im, ssm_log_step, ssm_b_re, ssm_b_im, ssm_c_re, ssm_c_im,
           ssm_d, ssm_w_glu_a, ssm_w_glu_b,
           norm_kv, w_k, w_v, attn_w_q, attn_w_o, norm_final):
    n_seq, seq, d = x_prompt.shape
    n_dec, dec_seq, _ = x_sample.shape
    assert dec_seq == 1 and norm_ffn1.shape[0] == 2 and ssm_a_re.shape[0] == 1
    n_pool, page_size, n_heads, head_dim = cache_k.shape
    assert (page_table.shape[1] * page_size) % MOBA_BLOCK == 0

    bf = lambda w: w.astype(BF16)
    ffn1 = (norm_ffn1, bf(w_ffn1_gate), bf(w_ffn1_up), bf(w_ffn1_down))
    ffn2 = (norm_ffn2, bf(w_ffn2_gate), bf(w_ffn2_up), bf(w_ffn2_down))
    glu_a, glu_b = bf(ssm_w_glu_a[0]), bf(ssm_w_glu_b[0])
    wk, wv, wq, wo = bf(w_k), bf(w_v), bf(attn_w_q[0]), bf(attn_w_o[0])
    opw, bst, cst, lb, m0, b0, c1 = ssm_weights(ssm_a_re[0], ssm_a_im[0], ssm_log_step[0], ssm_b_re[0],
                                                ssm_b_im[0], ssm_c_re[0], ssm_c_im[0])

    n_host = 4
    assert n_dec % n_host == 0
    block_sums = []

    def prompt_ffn(x, w, layer, **post):
        first = len(block_sums) * (n_dec // n_host)
        *out, sums = half_ffn_with_page_sums(x, *w, layer, cache_k, page_table, first, n_dec // n_host, **post)
        block_sums.append(sums)
        return out[0] if len(out) == 1 else out

    xp = x_prompt.reshape(n_seq * seq, d)
    xp, u = prompt_ffn(xp, ffn1, 0, post="x+planes", g_post=norm_mix[0])
    y, p_re, p_im = s5_prompt(u, n_seq, opw, bst, cst, lb)
    xp = ssm_glu(y, xp, norm_mix[0], ssm_d[0], glu_a, glu_b)
    xp = prompt_ffn(xp, ffn2, 0)
    k_p, v_p, kb_p, vb_p, ksum_p = shared_kv(xp, norm_kv, wk, wv, True)
    xp = prompt_ffn(xp, ffn1, 1)
    q_ext = query_select(xp, norm_mix[1], wq, ksum_p, n_seq)
    attn_p = moba_prompt(q_ext, kb_p.reshape(n_seq, seq, d), vb_p.reshape(n_seq, seq, d))
    xp = proj_residual(attn_p, wo, xp)
    y_prompt = prompt_ffn(xp, ffn2, 1, post="normed", g_post=norm_final).reshape(n_seq, seq, d)

    xs = x_sample.reshape(n_dec, d)
    xs = half_ffn(xs, *ffn1, 0)
    y, s_re, s_im = s5_step(rmsnorm(xs, norm_mix[0], F32), state_ssm_re[0], state_ssm_im[0], lb, m0, b0, c1)
    xs = ssm_glu(y, xs, norm_mix[0], ssm_d[0], glu_a, glu_b)
    xs = half_ffn(xs, *ffn2, 0)
    k_s, v_s, _, _ = shared_kv(xs, norm_kv, wk, wv, False)
    xs = half_ffn(xs, *ffn1, 1)
    q_s = norm_proj(xs, norm_mix[1], wq)
    per_head = lambda a: a.reshape(n_dec, n_heads, head_dim)
    sel = sample_select(per_head(q_s), jnp.concatenate(block_sums, axis=0))
    attn_s = moba_sample(per_head(q_s), per_head(k_s), per_head(v_s), cache_k, cache_v, page_table, sel)
    xs = proj_residual(attn_s.reshape(n_dec, d).astype(BF16), wo, xs)
    xs = half_ffn(xs, *ffn2, 1)
    y_sample = rmsnorm(xs, norm_final, F32).reshape(n_dec, 1, d)

    heads = lambda a, b, s: a.reshape(b, s, n_heads, head_dim)
    return (y_prompt, y_sample, p_re[None], p_im[None], s_re[None], s_im[None],
            heads(k_p, n_seq, seq), heads(v_p, n_seq, seq), heads(k_s, n_dec, 1), heads(v_s, n_dec, 1))
```

```python
import functools

import jax
import jax.numpy as jnp
from jax import lax
from jax.experimental import pallas as pl
from jax.experimental.pallas import tpu as pltpu

F32 = jnp.float32
BF16 = jnp.bfloat16

D_MODEL = 1024
N_GROUPS = 64
GROUP_SIZE = 16
STATE_DIM = 64
N_HEADS = 8
HEAD_DIM = 128
MOBA_BLOCK = 256
TOP_K = 3
PAGE_SIZE = 128
RMS_EPS = 1e-6
ATTN_SCALE = HEAD_DIM ** -0.5
LOG2_E = 1.4426950408889634
MASK_VALUE = -1e30

LANES = 128
SSM_CHUNK = 8
GROUPS_PER_BLOCK = LANES // GROUP_SIZE
BLOCK_STATE = GROUPS_PER_BLOCK * 2 * STATE_DIM
ROW_TILE = 512
FFN_PAGES_ROW_TILE = 256
SSM_CHUNK_TILE = 1024
SCAN_TILE = 128
MOBA_HEADS_PER_STEP = 8
MOBA_BLOCKS_PER_STEP = 4
VMEM_LIMIT = 56 * 1024 * 1024

_NT = (((1,), (1,)), ((), ()))


def _params(*sem):
    return pltpu.CompilerParams(dimension_semantics=sem, vmem_limit_bytes=VMEM_LIMIT)


def _resident(shape):
    zeros = (0,) * len(shape)
    return pl.BlockSpec(shape, lambda *_: zeros, pipeline_mode=pl.Buffered(1))


def _rms(x, g):
    inv = lax.rsqrt(jnp.mean(x * x, axis=-1, keepdims=True) + RMS_EPS)
    return x * inv * g


def _row_tile(m):
    return ROW_TILE if m % ROW_TILE == 0 else m


def _ffn_value(x, g_ref, wg_ref, wu_ref, wd_ref):
    h = _rms(x, g_ref[...]).astype(BF16)
    a = jnp.dot(h, wg_ref[0], preferred_element_type=F32)
    b = jnp.dot(h, wu_ref[0], preferred_element_type=F32)
    act = (jax.nn.silu(a) * b).astype(BF16)
    return x + 0.5 * jnp.dot(act, wd_ref[0], preferred_element_type=F32)


def _ffn_kernel(x_ref, g_ref, wg_ref, wu_ref, wd_ref, o_ref):
    o_ref[...] = _ffn_value(x_ref[...], g_ref, wg_ref, wu_ref, wd_ref)


def _layer_weights(layer, w):
    return pl.BlockSpec((1,) + w.shape[1:], lambda *_: (layer, 0, 0), pipeline_mode=pl.Buffered(1))


def half_ffn(x, g, wg, wu, wd, layer):
    m, d = x.shape
    tm = _row_tile(m)
    row = pl.BlockSpec((tm, d), lambda i: (i, 0))
    lw = lambda w: _layer_weights(layer, w)
    return pl.pallas_call(
        _ffn_kernel,
        grid=(m // tm,),
        in_specs=[row, _resident((1, d)), lw(wg), lw(wu), lw(wd)],
        out_specs=row,
        out_shape=jax.ShapeDtypeStruct((m, d), F32),
        compiler_params=_params("parallel"),
        name="half_ffn",
    )(x, g[layer].reshape(1, d), wg, wu, wd)


_PRE_OPERANDS = {"none": 0, "proj": 2, "glu": 5}


def _ffn_pages_kernel(n_pages, pre, post, pt_ref, x_ref, g_ref, wg_ref, wu_ref, wd_ref, g2_ref, *refs):
    n_pre = _PRE_OPERANDS[pre]
    pre_refs, pages, outs = refs[:n_pre], refs[n_pre:n_pre + n_pages], refs[n_pre + n_pages:]
    sums_ref = outs[-1]
    per_block = MOBA_BLOCK // PAGE_SIZE
    for r in range(n_pages // per_block):
        total = jnp.sum(pages[r * per_block][0], axis=0)
        for j in range(1, per_block):
            total = total + jnp.sum(pages[r * per_block + j][0], axis=0)
        sums_ref[0, r] = total
    x = x_ref[...]
    if pre == "proj":
        a_ref, w_ref = pre_refs
        x = x + jnp.dot(a_ref[...], w_ref[...], preferred_element_type=F32)
    elif pre == "glu":
        x = _glu_value(pre_refs[0], x, *pre_refs[1:])
    x_new = _ffn_value(x, g_ref, wg_ref, wu_ref, wd_ref)
    if post == "normed":
        outs[0][...] = _rms(x_new, g2_ref[...])
        return
    outs[0][...] = x_new
    if post == "x+planes":
        u = _rms(x_new, g2_ref[...])
        for v in range(outs[1].shape[0]):
            outs[1][v] = u[:, v * LANES:(v + 1) * LANES]


def half_ffn_with_page_sums(x, g, wg, wu, wd, layer, cache_k, page_table, first_seq, n_seqs,
                            pre=("none",), post="x", g_post=None):
    m, d = x.shape
    tm = FFN_PAGES_ROW_TILE if m % FFN_PAGES_ROW_TILE == 0 else m
    steps = m // tm
    n_pool, ps, nh, hd = cache_k.shape
    n_pages = page_table.shape[1]
    per_block = MOBA_BLOCK // ps
    assert ps == PAGE_SIZE and (n_seqs * n_pages) % steps == 0
    per_step = n_seqs * n_pages // steps
    assert per_step % per_block == 0 and n_pages % per_step == 0
    steps_per_seq = n_pages // per_step
    row = pl.BlockSpec((tm, d), lambda i, pt: (i, 0))
    vec = pl.BlockSpec((1, d), lambda i, pt: (0, 0), pipeline_mode=pl.Buffered(1))
    mat = pl.BlockSpec((d, d), lambda i, pt: (0, 0), pipeline_mode=pl.Buffered(1))
    planes = pl.BlockSpec((d // LANES, tm, LANES), lambda i, pt: (0, i, 0))
    lw = lambda w: _layer_weights(layer, w)
    page = lambda r: pl.BlockSpec(
        (1, ps, nh, hd),
        lambda i, pt: (pt[first_seq + i // steps_per_seq, (i % steps_per_seq) * per_step + r], 0, 0, 0))
    kind, *pre_args = pre
    assert len(pre_args) == _PRE_OPERANDS[kind]
    if kind == "proj":
        pre_specs = [row, mat]
    elif kind == "glu":
        pre_specs = [planes, vec, vec, mat, mat]
        pre_args = [pre_args[0], pre_args[1].reshape(1, d), pre_args[2].reshape(1, d), *pre_args[3:]]
    else:
        pre_specs = []
    out_specs, out_shape = [row], [jax.ShapeDtypeStruct((m, d), F32)]
    if post == "x+planes":
        out_specs.append(planes)
        out_shape.append(jax.ShapeDtypeStruct((d // LANES, m, LANES), F32))
    out_specs.append(pl.BlockSpec((1, per_step // per_block, nh, hd),
                                  lambda i, pt: (i // steps_per_seq, i % steps_per_seq, 0, 0)))
    out_shape.append(jax.ShapeDtypeStruct((n_seqs, n_pages // per_block, nh, hd), F32))
    g_post = g[layer] if g_post is None else g_post
    return pl.pallas_call(
        functools.partial(_ffn_pages_kernel, per_step, kind, post),
        grid_spec=pltpu.PrefetchScalarGridSpec(
            num_scalar_prefetch=1,
            grid=(steps,),
            in_specs=[row, vec, lw(wg), lw(wu), lw(wd), vec] + pre_specs + [page(r) for r in range(per_step)],
            out_specs=out_specs,
        ),
        out_shape=out_shape,
        compiler_params=_params("parallel"),
        name="half_ffn_page_sums",
    )(page_table, x, g[layer].reshape(1, d), wg, wu, wd, g_post.reshape(1, d), *pre_args,
      *([cache_k] * per_step))


def _rmsnorm_kernel(x_ref, g_ref, o_ref):
    o_ref[...] = _rms(x_ref[...], g_ref[...]).astype(o_ref.dtype)


def rmsnorm(x, g, dtype):
    m, d = x.shape
    tm = _row_tile(m)
    row = pl.BlockSpec((tm, d), lambda i: (i, 0))
    return pl.pallas_call(
        _rmsnorm_kernel,
        grid=(m // tm,),
        in_specs=[row, _resident((1, d))],
        out_specs=row,
        out_shape=jax.ShapeDtypeStruct((m, d), dtype),
        compiler_params=_params("parallel"),
        name="rmsnorm",
    )(x, g.reshape(1, d))


def _glu_value(y_ref, x, g_ref, d_ref, wa_ref, wb_ref):
    u = _rms(x, g_ref[...])
    y = jnp.concatenate([y_ref[v] for v in range(y_ref.shape[0])], axis=1)
    act = jax.nn.gelu(y + d_ref[...] * u).astype(BF16)
    a = jnp.dot(act, wa_ref[...], preferred_element_type=F32)
    b = jnp.dot(act, wb_ref[...], preferred_element_type=F32)
    return x + a * jax.nn.sigmoid(b)


def _glu_kernel(y_ref, x_ref, g_ref, d_ref, wa_ref, wb_ref, o_ref):
    o_ref[...] = _glu_value(y_ref, x_ref[...], g_ref, d_ref, wa_ref, wb_ref)


def ssm_glu(y, x, g_mix, d_skip, wa, wb):
    m, d = x.shape
    tm = _row_tile(m)
    row = pl.BlockSpec((tm, d), lambda i: (i, 0))
    planes = pl.BlockSpec((d // LANES, tm, LANES), lambda i: (0, i, 0))
    return pl.pallas_call(
        _glu_kernel,
        grid=(m // tm,),
        in_specs=[planes, row, _resident((1, d)), _resident((1, d)), _resident((d, d)), _resident((d, d))],
        out_specs=row,
        out_shape=jax.ShapeDtypeStruct((m, d), F32),
        compiler_params=_params("parallel"),
        name="ssm_glu",
    )(y, x, g_mix.reshape(1, d), d_skip.reshape(1, d), wa, wb)


def _kv_kernel(n_sum, x_ref, g_ref, wk_ref, wv_ref, k_ref, v_ref, kb_ref, vb_ref, *sum_ref):
    h = _rms(x_ref[...], g_ref[...]).astype(BF16)
    k = jnp.dot(h, wk_ref[...], preferred_element_type=F32)
    v = jnp.dot(h, wv_ref[...], preferred_element_type=F32)
    k_ref[...] = k
    v_ref[...] = v
    kb_ref[...] = k.astype(BF16)
    vb_ref[...] = v.astype(BF16)
    for r in range(n_sum):
        sum_ref[0][r] = jnp.sum(k[r * MOBA_BLOCK:(r + 1) * MOBA_BLOCK], axis=0, keepdims=True)


def shared_kv(x, g, wk, wv, with_block_sums):
    m, d = x.shape
    tm = _row_tile(m)
    n_sum = tm // MOBA_BLOCK if with_block_sums else 0
    row = pl.BlockSpec((tm, d), lambda i: (i, 0))
    out_specs = [row, row, row, row]
    out_shape = [jax.ShapeDtypeStruct((m, d), F32)] * 2 + [jax.ShapeDtypeStruct((m, d), BF16)] * 2
    if with_block_sums:
        assert tm % MOBA_BLOCK == 0
        out_specs.append(pl.BlockSpec((n_sum, 1, d), lambda i: (i, 0, 0)))
        out_shape.append(jax.ShapeDtypeStruct((m // MOBA_BLOCK, 1, d), F32))
    return pl.pallas_call(
        functools.partial(_kv_kernel, n_sum),
        grid=(m // tm,),
        in_specs=[row, _resident((1, d)), _resident((d, d)), _resident((d, d))],
        out_specs=out_specs,
        out_shape=out_shape,
        compiler_params=_params("parallel"),
        name="shared_kv",
    )(x, g.reshape(1, d), wk, wv)


def _proj_residual_kernel(a_ref, w_ref, x_ref, o_ref):
    o_ref[...] = x_ref[...] + jnp.dot(a_ref[...], w_ref[...], preferred_element_type=F32)


def proj_residual(a, w, x):
    m, d = x.shape
    tm = _row_tile(m)
    row = pl.BlockSpec((tm, d), lambda i: (i, 0))
    return pl.pallas_call(
        _proj_residual_kernel,
        grid=(m // tm,),
        in_specs=[row, _resident((d, d)), row],
        out_specs=row,
        out_shape=jax.ShapeDtypeStruct((m, d), F32),
        compiler_params=_params("parallel"),
        name="proj_residual",
    )(a, w, x)


def _norm_proj_kernel(x_ref, g_ref, w_ref, o_ref):
    h = _rms(x_ref[...], g_ref[...]).astype(BF16)
    o_ref[...] = jnp.dot(h, w_ref[...], preferred_element_type=F32)


def norm_proj(x, g, w):
    m, d = x.shape
    tm = _row_tile(m)
    row = pl.BlockSpec((tm, d), lambda i: (i, 0))
    return pl.pallas_call(
        _norm_proj_kernel,
        grid=(m // tm,),
        in_specs=[row, _resident((1, d)), _resident((d, d))],
        out_specs=row,
        out_shape=jax.ShapeDtypeStruct((m, d), F32),
        compiler_params=_params("parallel"),
        name="norm_proj",
    )(x, g.reshape(1, d), w)


def _ssm_weights_kernel(are_ref, aim_ref, ls_ref, bre_ref, bim_ref, cre_ref, cim_ref,
                        opw_ref, bst_ref, cst_ref, lb_ref, m0_ref, b0_ref, c1_ref):
    p, gs, t_len = STATE_DIM, GROUP_SIZE, SSM_CHUNK
    zeros = lambda r, c: jnp.zeros((r, c), F32)
    for a in range(GROUPS_PER_BLOCK):
        lam_re = jnp.minimum(are_ref[a], -1e-4)
        lam_im = aim_ref[a]
        dt = jnp.exp(ls_ref[a])
        lag = lax.broadcasted_iota(jnp.int32, (t_len + 8, p), 0).astype(F32)
        mag = jnp.exp(lag * (lam_re * dt))
        ang = lag * (lam_im * dt)
        pw_re = mag * jnp.cos(ang)
        pw_im = mag * jnp.sin(ang)
        lb_re = pw_re[1:2]
        lb_im = pw_im[1:2]
        den = lam_re * lam_re + lam_im * lam_im
        nr = lb_re - 1.0
        coef_re = (nr * lam_re + lb_im * lam_im) / den
        coef_im = (lb_im * lam_re - nr * lam_im) / den
        bt_re = bre_ref[a]
        bt_im = bim_ref[a]
        bb_re = coef_re * bt_re - coef_im * bt_im
        bb_im = coef_re * bt_im + coef_im * bt_re
        bb_cat = jnp.concatenate([bb_re, bb_im], axis=1)
        c_re = cre_ref[a]
        c_im = cim_ref[a]

        def c_pow(t):
            re = c_re * pw_re[t:t + 1] - c_im * pw_im[t:t + 1]
            im = c_re * pw_im[t:t + 1] + c_im * pw_re[t:t + 1]
            return jnp.concatenate([re, -im], axis=1)

        def in_block_rows(x):
            parts = ([zeros(a * gs, x.shape[1])] if a else []) + [x]
            rest = LANES - (a + 1) * gs
            return jnp.concatenate(parts + ([zeros(rest, x.shape[1])] if rest else []), axis=0)

        def in_block_cols(x):
            parts = ([zeros(gs, a * 2 * p)] if a else []) + [x]
            rest = BLOCK_STATE - (a + 1) * 2 * p
            return jnp.concatenate(parts + ([zeros(gs, rest)] if rest else []), axis=1)

        cp = [c_pow(t) for t in range(t_len + 1)]
        spread = jnp.concatenate([in_block_rows(cp[t]) for t in range(t_len)], axis=0)
        mt = lax.dot_general(bb_cat, spread, _NT, precision=lax.Precision.HIGHEST,
                             preferred_element_type=F32)
        for s in range(t_len):
            rows = slice(s * LANES + a * gs, s * LANES + (a + 1) * gs)
            shifted = mt if s == 0 else jnp.concatenate(
                [zeros(gs, s * LANES), mt[:, :(t_len - s) * LANES]], axis=1)
            opw_ref[0, rows, :] = shifted.astype(opw_ref.dtype)
            q = t_len - 1 - s
            re = bb_re * pw_re[q:q + 1] - bb_im * pw_im[q:q + 1]
            im = bb_re * pw_im[q:q + 1] + bb_im * pw_re[q:q + 1]
            bst_ref[0, rows, :] = in_block_cols(jnp.concatenate([re, im], axis=1)).astype(bst_ref.dtype)
            cst_ref[0, rows, :] = in_block_cols(cp[s + 1]).astype(cst_ref.dtype)
        lb_ref[a, 0:1, :] = jnp.concatenate([pw_re[t_len:t_len + 1], pw_im[t_len:t_len + 1]], axis=1)
        lb_ref[a, 1:2, :] = jnp.concatenate([lb_re, lb_im], axis=1)
        lb_ref[a, 2:8, :] = zeros(6, 2 * p)
        m0_ref[a] = mt[:, :LANES]
        b0_ref[a] = bb_cat
        c1_ref[a] = cp[1]


def ssm_weights(a_re, a_im, log_step, b_re, b_im, c_re, c_im):
    g, p, gs, gb = N_GROUPS, STATE_DIM, GROUP_SIZE, GROUPS_PER_BLOCK
    tl = SSM_CHUNK * LANES
    grp = lambda r, c: pl.BlockSpec((gb, r, c), lambda i: (i, 0, 0))
    blk = lambda r, c: pl.BlockSpec((1, r, c), lambda i: (i, 0, 0))
    return pl.pallas_call(
        _ssm_weights_kernel,
        grid=(g // gb,),
        in_specs=[grp(1, p), grp(1, p), grp(1, 1), grp(gs, p), grp(gs, p), grp(gs, p), grp(gs, p)],
        out_specs=[blk(tl, tl), blk(tl, BLOCK_STATE), blk(tl, BLOCK_STATE),
                   grp(8, 2 * p), grp(gs, LANES), grp(gs, 2 * p), grp(gs, 2 * p)],
        out_shape=[jax.ShapeDtypeStruct((g // gb, tl, tl), BF16),
                   jax.ShapeDtypeStruct((g // gb, tl, BLOCK_STATE), BF16),
                   jax.ShapeDtypeStruct((g // gb, tl, BLOCK_STATE), BF16),
                   jax.ShapeDtypeStruct((g, 8, 2 * p), F32),
                   jax.ShapeDtypeStruct((g, gs, LANES), F32),
                   jax.ShapeDtypeStruct((g, gs, 2 * p), F32),
                   jax.ShapeDtypeStruct((g, gs, 2 * p), F32)],
        compiler_params=_params("parallel"),
        name="ssm_weights",
    )(a_re.reshape(g, 1, p), a_im.reshape(g, 1, p), log_step.reshape(g, 1, 1),
      b_re.transpose(0, 2, 1), b_im.transpose(0, 2, 1), c_re, c_im)


def _lane_block(u_ref, n_chunks):
    return jnp.concatenate(
        [u_ref[0, pl.ds(t, n_chunks, stride=SSM_CHUNK), :] for t in range(SSM_CHUNK)], axis=1).astype(BF16)


def _chunk_state_kernel(n_chunks, u_ref, bst_ref, e_ref):
    e_ref[0] = jnp.dot(_lane_block(u_ref, n_chunks), bst_ref[0], preferred_element_type=F32)


def _chunk_tile(nc):
    return SSM_CHUNK_TILE if nc % SSM_CHUNK_TILE == 0 else nc


def chunk_states(u, bst):
    nv, m, _ = u.shape
    nc = m // SSM_CHUNK
    nct = _chunk_tile(nc)
    return pl.pallas_call(
        functools.partial(_chunk_state_kernel, nct),
        grid=(nv, nc // nct),
        in_specs=[pl.BlockSpec((1, nct * SSM_CHUNK, LANES), lambda v, i: (v, i, 0)),
                  pl.BlockSpec((1,) + bst.shape[1:], lambda v, i: (v, 0, 0))],
        out_specs=pl.BlockSpec((1, nct, BLOCK_STATE), lambda v, i: (v, i, 0)),
        out_shape=jax.ShapeDtypeStruct((nv, nc, BLOCK_STATE), F32),
        compiler_params=_params("parallel", "parallel"),
        name="ssm_chunk_states",
    )(u, bst)


def _swap_halves(x):
    half = STATE_DIM
    n = x.shape[-1]
    lane = lax.broadcasted_iota(jnp.int32, x.shape, x.ndim - 1)
    return jnp.where(lane % (2 * half) < half, pltpu.roll(x, n - half, axis=x.ndim - 1),
                     pltpu.roll(x, half, axis=x.ndim - 1))


def _scan_kernel(n_steps, e_ref, h0_ref, a_ref, hin_ref, hfin_ref, h_scr, hs_scr):
    @pl.when(pl.program_id(1) == 0)
    def _():
        h_scr[...] = h0_ref[0]
        hs_scr[...] = _swap_halves(h0_ref[0])

    a_same = a_ref[0]
    a_cross = a_ref[1]
    a_cross_s = a_ref[2]

    def body(t, carry):
        h, hs = carry
        e = e_ref[t]
        es = _swap_halves(e)
        hin_ref[t] = h.astype(hin_ref.dtype)
        return a_same * h + a_cross * hs + e, a_same * hs + a_cross_s * h + es

    h, hs = lax.fori_loop(0, n_steps, body, (h_scr[...], hs_scr[...]), unroll=8)
    h_scr[...] = h
    hs_scr[...] = hs
    hfin_ref[0] = h


def chunk_scan(e, h0, a):
    nc, nv, w = e.shape
    n_seq = h0.shape[0]
    per_seq = nc // n_seq
    ct = SCAN_TILE if per_seq % SCAN_TILE == 0 else per_seq
    n_ct = per_seq // ct
    tile = pl.BlockSpec((ct, nv, w), lambda b, c: (b * n_ct + c, 0, 0))
    seq = pl.BlockSpec((1, nv, w), lambda b, c: (b, 0, 0))
    return pl.pallas_call(
        functools.partial(_scan_kernel, ct),
        grid=(n_seq, n_ct),
        in_specs=[tile, seq, _resident((3, nv, w))],
        out_specs=[tile, seq],
        out_shape=[jax.ShapeDtypeStruct((nc, nv, w), BF16), jax.ShapeDtypeStruct((n_seq, nv, w), F32)],
        scratch_shapes=[pltpu.VMEM((nv, w), F32), pltpu.VMEM((nv, w), F32)],
        compiler_params=_params("parallel", "arbitrary"),
        name="ssm_chunk_scan",
    )(e, h0, a)


def _chunk_out_kernel(n_chunks, u_ref, hin_ref, opw_ref, cst_ref, y_ref):
    w = _lane_block(u_ref, n_chunks)
    hin = hin_ref[0]
    tile = 2 * LANES
    for c in range(SSM_CHUNK * LANES // tile):
        cols = slice(c * tile, (c + 1) * tile)
        k = (c + 1) * tile
        y = jnp.dot(w[:, :k], opw_ref[0, :k, cols], preferred_element_type=F32)
        y = y + lax.dot_general(hin, cst_ref[0, cols, :], _NT, preferred_element_type=F32)
        for j in range(tile // LANES):
            t = c * (tile // LANES) + j
            y_ref[0, pl.ds(t, n_chunks, stride=SSM_CHUNK), :] = y[:, j * LANES:(j + 1) * LANES]


def chunk_outputs(u, hin, opw, cst):
    m = u.shape[1]
    nv, nc, w = hin.shape
    nct = _chunk_tile(nc)
    rows = nct * SSM_CHUNK
    per_block = lambda a: pl.BlockSpec((1,) + a.shape[1:], lambda v, i: (v, 0, 0))
    return pl.pallas_call(
        functools.partial(_chunk_out_kernel, nct),
        grid=(nv, nc // nct),
        in_specs=[pl.BlockSpec((1, rows, LANES), lambda v, i: (v, i, 0)),
                  pl.BlockSpec((1, nct, w), lambda v, i: (v, i, 0)), per_block(opw), per_block(cst)],
        out_specs=pl.BlockSpec((1, rows, LANES), lambda v, i: (v, i, 0)),
        out_shape=jax.ShapeDtypeStruct((nv, m, LANES), F32),
        compiler_params=_params("parallel", "parallel"),
        name="ssm_chunk_outputs",
    )(u, hin, opw, cst)


def s5_prompt(u, n_seq, opw, bst, cst, lb):
    g, p = N_GROUPS, STATE_DIM
    nv = g // GROUPS_PER_BLOCK
    e = chunk_states(u, bst).transpose(1, 0, 2)
    lt_re, lt_im = lb[:, 0, :p], lb[:, 0, p:]
    per_block = lambda *halves: jnp.concatenate(halves, axis=1).reshape(nv, BLOCK_STATE)
    a = jnp.stack([per_block(lt_re, lt_re), per_block(-lt_im, lt_im), per_block(lt_im, -lt_im)])
    hin, hfin = chunk_scan(e, jnp.zeros((n_seq, nv, BLOCK_STATE), F32), a)
    y = chunk_outputs(u, hin.transpose(1, 0, 2), opw, cst)
    hfin = hfin.reshape(n_seq, g, 2 * p)
    return y, hfin[:, :, :p], hfin[:, :, p:]


def _ssm_step_kernel(u_ref, hre_ref, him_ref, m0_ref, bre_ref, bim_ref, cre_ref, cim_ref,
                     lre_ref, lim_ref, y_ref, ore_ref, oim_ref):
    u = u_ref[...]
    h_re, h_im = hre_ref[...], him_ref[...]
    l_re, l_im = lre_ref[...], lim_ref[...]
    bmm = lambda a, b: jnp.einsum("gbk,gkn->gbn", a, b, preferred_element_type=F32)
    bmm_nt = lambda a, b: jnp.einsum("gbk,gnk->gbn", a, b, preferred_element_type=F32)
    ore_ref[...] = l_re * h_re - l_im * h_im + bmm(u, bre_ref[...])
    oim_ref[...] = l_re * h_im + l_im * h_re + bmm(u, bim_ref[...])
    y_ref[...] = bmm(u, m0_ref[...]) + bmm_nt(h_re, cre_ref[...]) + bmm_nt(h_im, cim_ref[...])


def s5_step(u, h_re, h_im, lb, m0, b0, c1):
    b, d = u.shape
    g, gs, p = N_GROUPS, GROUP_SIZE, STATE_DIM
    m0t = m0.reshape(g, gs, GROUPS_PER_BLOCK, gs).sum(axis=2)
    y, o_re, o_im = pl.pallas_call(
        _ssm_step_kernel,
        out_shape=[jax.ShapeDtypeStruct((g, b, gs), F32),
                   jax.ShapeDtypeStruct((g, b, p), F32), jax.ShapeDtypeStruct((g, b, p), F32)],
        compiler_params=pltpu.CompilerParams(vmem_limit_bytes=VMEM_LIMIT),
        name="ssm_step",
    )(u.reshape(b, g, gs).transpose(1, 0, 2), h_re.transpose(1, 0, 2), h_im.transpose(1, 0, 2),
      m0t, b0[:, :, :p], b0[:, :, p:], c1[:, :, :p], c1[:, :, p:], lb[:, 1:2, :p], lb[:, 1:2, p:])
    y = y.reshape(d // LANES, GROUPS_PER_BLOCK, b, gs).transpose(0, 2, 1, 3).reshape(d // LANES, b, LANES)
    return y, o_re.transpose(1, 0, 2), o_im.transpose(1, 0, 2)


def _select_top_blocks(gate, blk):
    bias = jnp.full(gate.shape, MASK_VALUE, F32)
    for _ in range(TOP_K):
        m = jnp.max(gate, axis=0, keepdims=True)
        first = jnp.min(jnp.where(gate == m, blk, float(gate.shape[0])), axis=0, keepdims=True)
        hit = blk == first
        bias = jnp.where(hit, jnp.where(m > -jnp.inf, 0.0, bias), bias)
        gate = jnp.where(hit, -jnp.inf, gate)
    return bias


def _query_select_kernel(x_ref, g_ref, wq_ref, ks_ref, o_ref):
    own = pl.program_id(1)
    h = _rms(x_ref[...], g_ref[...]).astype(BF16)
    q = jnp.dot(h, wq_ref[...], preferred_element_type=F32)
    qb = q.astype(BF16)
    means = (ks_ref[0] * (1.0 / MOBA_BLOCK)).astype(BF16)
    nb8, tq = means.shape[0], q.shape[0]
    blk = lax.broadcasted_iota(jnp.int32, (nb8, tq), 0)
    past = blk < own
    blk = blk.astype(F32)
    unused = jnp.full((LANES - nb8, tq), MASK_VALUE, F32)
    for hh in range(N_HEADS):
        cols = slice(hh * HEAD_DIM, (hh + 1) * HEAD_DIM)
        gate = lax.dot_general(means[:, cols], qb[:, cols], _NT, preferred_element_type=F32)
        bias = _select_top_blocks(jnp.where(past, gate, -jnp.inf), blk)
        o_ref[0, hh, :, :HEAD_DIM] = (q[:, cols] * (ATTN_SCALE * LOG2_E)).astype(BF16)
        o_ref[0, hh, :, HEAD_DIM:] = jnp.concatenate([bias, unused], axis=0).T.astype(BF16)


def query_select(x, g, wq, block_sums, n_seq):
    m, d = x.shape
    seq = m // n_seq
    nb = seq // MOBA_BLOCK
    nb8 = -(-nb // 8) * 8
    assert seq % MOBA_BLOCK == 0 and nb8 <= LANES
    sums = jnp.pad(block_sums.reshape(n_seq, nb, d), ((0, 0), (0, nb8 - nb), (0, 0)))
    return pl.pallas_call(
        _query_select_kernel,
        grid=(n_seq, nb),
        in_specs=[pl.BlockSpec((MOBA_BLOCK, d), lambda b, i: (b * nb + i, 0)),
                  _resident((1, d)), _resident((d, d)),
                  pl.BlockSpec((1, nb8, d), lambda b, i: (b, 0, 0))],
        out_specs=pl.BlockSpec((1, N_HEADS, MOBA_BLOCK, 2 * HEAD_DIM), lambda b, i: (b, 0, i, 0)),
        out_shape=jax.ShapeDtypeStruct((n_seq, N_HEADS, seq, 2 * HEAD_DIM), BF16),
        compiler_params=_params("parallel", "parallel"),
        name="moba_query_select",
    )(x, g.reshape(1, d), wq, sums)


def _moba_kernel(q_ref, k_ref, v_ref, o_ref):
    i = pl.program_id(2)
    tq = MOBA_BLOCK
    span = MOBA_BLOCKS_PER_STEP
    n_heads = q_ref.shape[1]
    cols = [slice(h * HEAD_DIM, (h + 1) * HEAD_DIM) for h in range(n_heads)]
    own = pl.multiple_of(i * tq, tq)
    row = lax.broadcasted_iota(jnp.int32, (tq, tq), 0)
    col = lax.broadcasted_iota(jnp.int32, (tq, tq), 1)
    causal = col <= row
    ones = jnp.ones((span * tq, HEAD_DIM), BF16)
    state = []
    for h in range(n_heads):
        s = lax.dot_general(q_ref[0, h, :, :HEAD_DIM], k_ref[0, pl.ds(own, tq), cols[h]], _NT,
                            preferred_element_type=F32)
        s = jnp.where(causal, s, MASK_VALUE)
        m = jnp.max(s, axis=1, keepdims=True)
        p = jnp.exp2(s - m)
        v_own = jnp.concatenate([v_ref[0, pl.ds(own, tq), cols[h]], ones[:tq]], axis=1)
        state += [m, jnp.dot(p.astype(BF16), v_own, preferred_element_type=F32)]
    lane = lax.broadcasted_iota(jnp.int32, (span * tq, HEAD_DIM), 1)
    block = lax.broadcasted_iota(jnp.int32, (span * tq, HEAD_DIM), 0) // tq

    def body(g, carry):
        off = pl.multiple_of(g * (span * tq), span * tq)
        one_hot = jnp.where(lane == g * span + block, 1.0, 0.0).astype(BF16)
        out = []
        for h in range(n_heads):
            m, acc = carry[2 * h:2 * h + 2]
            k_ext = jnp.concatenate([k_ref[0, pl.ds(off, span * tq), cols[h]], one_hot], axis=1)
            s = lax.dot_general(q_ref[0, h], k_ext, _NT, preferred_element_type=F32)
            m_new = jnp.maximum(m, jnp.max(s, axis=1, keepdims=True))
            p = jnp.exp2(s - m_new)
            v_ext = jnp.concatenate([v_ref[0, pl.ds(off, span * tq), cols[h]], ones], axis=1)
            acc = jnp.exp2(m - m_new) * acc + jnp.dot(p.astype(BF16), v_ext, preferred_element_type=F32)
            out += [m_new, acc]
        return tuple(out)

    state = lax.fori_loop(0, (i + span - 1) // span, body, tuple(state))
    for h in range(n_heads):
        acc = state[2 * h + 1]
        o_ref[0, :, cols[h]] = (acc[:, :HEAD_DIM] / acc[:, HEAD_DIM:]).astype(o_ref.dtype)


def moba_prompt(q_ext, k_bf, v_bf):
    n_seq, _, seq, _ = q_ext.shape
    d = k_bf.shape[-1]
    nb = seq // MOBA_BLOCK
    hb = MOBA_HEADS_PER_STEP
    assert seq % (MOBA_BLOCKS_PER_STEP * MOBA_BLOCK) == 0
    kv = pl.BlockSpec((1, seq, hb * HEAD_DIM), lambda b, h, i: (b, 0, h), pipeline_mode=pl.Buffered(1))
    out = pl.pallas_call(
        _moba_kernel,
        grid=(n_seq, N_HEADS // hb, nb),
        in_specs=[pl.BlockSpec((1, hb, MOBA_BLOCK, 2 * HEAD_DIM), lambda b, h, i: (b, h, i, 0)), kv, kv],
        out_specs=pl.BlockSpec((1, MOBA_BLOCK, hb * HEAD_DIM), lambda b, h, i: (b, i, h)),
        out_shape=jax.ShapeDtypeStruct((n_seq, seq, d), BF16),
        compiler_params=_params("parallel", "parallel", "arbitrary"),
        name="moba_prompt",
    )(q_ext, k_bf, v_bf)
    return out.reshape(n_seq * seq, d)


def _sample_select_kernel(q_ref, sums_ref, o_ref):
    prod = q_ref[0] * (sums_ref[0] * (1.0 / MOBA_BLOCK))
    gate = jnp.sum(prod, axis=2, keepdims=True)
    nb = gate.shape[0]
    blk = lax.broadcasted_iota(jnp.int32, gate.shape, 0).astype(F32)
    for j in range(TOP_K):
        m = jnp.max(gate, axis=0, keepdims=True)
        first = jnp.min(jnp.where(gate == m, blk, float(nb - 1)), axis=0, keepdims=True)
        o_ref[0, j] = jnp.broadcast_to(first[0], o_ref.shape[2:]).astype(jnp.int32)
        gate = jnp.where(blk == first, -jnp.inf, gate)


def sample_select(q, sums):
    b, nh, hd = q.shape
    nb = sums.shape[1]
    assert nb >= TOP_K
    out = pl.pallas_call(
        _sample_select_kernel,
        grid=(b,),
        in_specs=[pl.BlockSpec((1, nh, hd), lambda s: (s, 0, 0)),
                  pl.BlockSpec((1, nb, nh, hd), lambda s: (s, 0, 0, 0))],
        out_specs=pl.BlockSpec((1, TOP_K, nh, hd), lambda s: (s, 0, 0, 0)),
        out_shape=jax.ShapeDtypeStruct((b, TOP_K, nh, hd), jnp.int32),
        compiler_params=_params("parallel"),
        name="sample_select",
    )(q, sums)
    return out[:, :, :, 0].transpose(0, 2, 1)


def _sample_attn_kernel(n_pages, sel_ref, pt_ref, q_ref, kn_ref, vn_ref, ck_ref, cv_ref, o_ref,
                        kbuf, vbuf, sem):
    step = pl.program_id(0)
    per_block = MOBA_BLOCK // PAGE_SIZE

    def copies(seq, slot):
        out = []
        for h in range(N_HEADS):
            for j in range(n_pages):
                page = pt_ref[seq, sel_ref[seq, h, j // per_block] * per_block + j % per_block]
                out.append(pltpu.make_async_copy(ck_ref.at[page, :, h, :], kbuf.at[slot, h, j], sem.at[slot]))
                out.append(pltpu.make_async_copy(cv_ref.at[page, :, h, :], vbuf.at[slot, h, j], sem.at[slot]))
        return out

    slot = step % 2

    @pl.when(step == 0)
    def _():
        for c in copies(0, 0):
            c.start()

    @pl.when(step + 1 < pl.num_programs(0))
    def _():
        for c in copies(step + 1, 1 - slot):
            c.start()

    for c in copies(step, slot):
        c.wait()

    q = q_ref[0] * ATTN_SCALE
    rows = []
    for h in range(N_HEADS):
        qh = q[h:h + 1]
        s_new = jnp.sum(qh * kn_ref[0, h:h + 1], axis=1, keepdims=True)
        scores = [jnp.sum(kbuf[slot, h, j] * qh, axis=1, keepdims=True) for j in range(n_pages)]
        m = s_new
        for s in scores:
            m = jnp.maximum(m, jnp.max(s, axis=0, keepdims=True))
        p_new = jnp.exp(s_new - m)
        l = p_new
        acc = p_new * vn_ref[0, h:h + 1]
        for j, s in enumerate(scores):
            p = jnp.exp(s - m)
            l = l + jnp.sum(p, axis=0, keepdims=True)
            acc = acc + jnp.sum(p * vbuf[slot, h, j], axis=0, keepdims=True)
        rows.append(acc / l)
    o_ref[0] = jnp.concatenate(rows, axis=0)


def moba_sample(q, k_new, v_new, cache_k, cache_v, page_table, sel):
    b, nh, hd = q.shape
    ps = cache_k.shape[1]
    n_pages = TOP_K * (MOBA_BLOCK // ps)
    seq = pl.BlockSpec((1, nh, hd), lambda s, sel_ref, pt_ref: (s, 0, 0))
    hbm = pl.BlockSpec(memory_space=pl.ANY)
    return pl.pallas_call(
        functools.partial(_sample_attn_kernel, n_pages),
        grid_spec=pltpu.PrefetchScalarGridSpec(
            num_scalar_prefetch=2,
            grid=(b,),
            in_specs=[seq, seq, seq, hbm, hbm],
            out_specs=seq,
            scratch_shapes=[pltpu.VMEM((2, nh, n_pages, ps, hd), F32),
                            pltpu.VMEM((2, nh, n_pages, ps, hd), F32),
                            pltpu.SemaphoreType.DMA((2,))],
        ),
        out_shape=jax.ShapeDtypeStruct((b, nh, hd), F32),
        compiler_params=_params("arbitrary"),
        name="moba_sample",
    )(sel, page_table, q, k_new, v_new, cache_k, cache_v)


def kernel(x_prompt, x_sample, state_ssm_re, state_ssm_im, cache_k, cache_v, page_table,
           norm_ffn1, w_ffn1_gate, w_ffn1_up, w_ffn1_down, norm_mix,
           norm_ffn2, w_ffn2_gate, w_ffn2_up, w_ffn2_down,
           ssm_a_re, ssm_a_im, ssm_log_step, ssm_b_re, ssm_b_im, ssm_c_re, ssm_c_im,
           ssm_d, ssm_w_glu_a, ssm_w_glu_b,
           norm_kv, w_k, w_v, attn_w_q, attn_w_o, norm_final):
    n_seq, seq, d = x_prompt.shape
    n_dec, dec_seq, _ = x_sample.shape
    assert dec_seq == 1 and norm_ffn1.shape[0] == 2 and ssm_a_re.shape[0] == 1
    n_pool, page_size, n_heads, head_dim = cache_k.shape
    assert (page_table.shape[1] * page_size) % MOBA_BLOCK == 0

    bf = lambda w: w.astype(BF16)
    ffn1 = (norm_ffn1, bf(w_ffn1_gate), bf(w_ffn1_up), bf(w_ffn1_down))
    ffn2 = (norm_ffn2, bf(w_ffn2_gate), bf(w_ffn2_up), bf(w_ffn2_down))
    glu_a, glu_b = bf(ssm_w_glu_a[0]), bf(ssm_w_glu_b[0])
    wk, wv, wq, wo = bf(w_k), bf(w_v), bf(attn_w_q[0]), bf(attn_w_o[0])
    opw, bst, cst, lb, m0, b0, c1 = ssm_weights(ssm_a_re[0], ssm_a_im[0], ssm_log_step[0], ssm_b_re[0],
                                                ssm_b_im[0], ssm_c_re[0], ssm_c_im[0])

    n_host = 4
    assert n_dec % n_host == 0
    block_sums = []

    def prompt_ffn(x, w, layer, **post):
        first = len(block_sums) * (n_dec // n_host)
        *out, sums = half_ffn_with_page_sums(x, *w, layer, cache_k, page_table, first, n_dec // n_host, **post)
        block_sums.append(sums)
        return out[0] if len(out) == 1 else out

    xp = x_prompt.reshape(n_seq * seq, d)
    xp, u = prompt_ffn(xp, ffn1, 0, post="x+planes", g_post=norm_mix[0])
    y, p_re, p_im = s5_prompt(u, n_seq, opw, bst, cst, lb)
    xp = prompt_ffn(xp, ffn2, 0, pre=("glu", y, norm_mix[0], ssm_d[0], glu_a, glu_b))
    k_p, v_p, kb_p, vb_p, ksum_p = shared_kv(xp, norm_kv, wk, wv, True)
    xp = prompt_ffn(xp, ffn1, 1)
    q_ext = query_select(xp, norm_mix[1], wq, ksum_p, n_seq)
    attn_p = moba_prompt(q_ext, kb_p.reshape(n_seq, seq, d), vb_p.reshape(n_seq, seq, d))
    y_prompt = prompt_ffn(xp, ffn2, 1, pre=("proj", attn_p, wo), post="normed",
                          g_post=norm_final).reshape(n_seq, seq, d)

    xs = x_sample.reshape(n_dec, d)
    xs = half_ffn(xs, *ffn1, 0)
    y, s_re, s_im = s5_step(rmsnorm(xs, norm_mix[0], F32), state_ssm_re[0], state_ssm_im[0], lb, m0, b0, c1)
    xs = ssm_glu(y, xs, norm_mix[0], ssm_d[0], glu_a, glu_b)
    xs = half_ffn(xs, *ffn2, 0)
    k_s, v_s, _, _ = shared_kv(xs, norm_kv, wk, wv, False)
    xs = half_ffn(xs, *ffn1, 1)
    q_s = norm_proj(xs, norm_mix[1], wq)
    per_head = lambda a: a.reshape(n_dec, n_heads, head_dim)
    sel = sample_select(per_head(q_s), jnp.concatenate(block_sums, axis=0))
    attn_s = moba_sample(per_head(q_s), per_head(k_s), per_head(v_s), cache_k, cache_v, page_table, sel)
    xs = proj_residual(attn_s.reshape(n_dec, d).astype(BF16), wo, xs)
    xs = half_ffn(xs, *ffn2, 1)
    y_sample = rmsnorm(xs, norm_final, F32).reshape(n_dec, 1, d)

    heads = lambda a, b, s: a.reshape(b, s, n_heads, head_dim)
    return (y_prompt, y_sample, p_re[None], p_im[None], s_re[None], s_im[None],
            heads(k_p, n_seq, seq), heads(v_p, n_seq, seq), heads(k_s, n_dec, 1), heads(v_s, n_dec, 1))
```

```python
import functools

import jax
import jax.numpy as jnp
from jax import lax
from jax.experimental import pallas as pl
from jax.experimental.pallas import tpu as pltpu

F32 = jnp.float32
BF16 = jnp.bfloat16

D_MODEL = 1024
N_GROUPS = 64
GROUP_SIZE = 16
STATE_DIM = 64
N_HEADS = 8
HEAD_DIM = 128
MOBA_BLOCK = 256
TOP_K = 3
PAGE_SIZE = 128
RMS_EPS = 1e-6
ATTN_SCALE = HEAD_DIM ** -0.5
LOG2_E = 1.4426950408889634
MASK_VALUE = -1e30

LANES = 128
SSM_CHUNK = 8
GROUPS_PER_BLOCK = LANES // GROUP_SIZE
BLOCK_STATE = GROUPS_PER_BLOCK * 2 * STATE_DIM
ROW_TILE = 512
FFN_PAGES_ROW_TILE = 256
SSM_CHUNK_TILE = 1024
SCAN_TILE = 128
MOBA_HEADS_PER_STEP = 8
SELECT_SEQS_PER_STEP = 8
MOBA_BLOCKS_PER_STEP = 4
VMEM_LIMIT = 56 * 1024 * 1024

_NT = (((1,), (1,)), ((), ()))


def _params(*sem):
    return pltpu.CompilerParams(dimension_semantics=sem, vmem_limit_bytes=VMEM_LIMIT)


def _resident(shape):
    zeros = (0,) * len(shape)
    return pl.BlockSpec(shape, lambda *_: zeros, pipeline_mode=pl.Buffered(1))


def _rms(x, g):
    inv = lax.rsqrt(jnp.mean(x * x, axis=-1, keepdims=True) + RMS_EPS)
    return x * inv * g


def _row_tile(m):
    return ROW_TILE if m % ROW_TILE == 0 else m


def _ffn_value(x, g_ref, wg_ref, wu_ref, wd_ref):
    h = _rms(x, g_ref[...]).astype(BF16)
    a = jnp.dot(h, wg_ref[0], preferred_element_type=F32)
    b = jnp.dot(h, wu_ref[0], preferred_element_type=F32)
    act = (jax.nn.silu(a) * b).astype(BF16)
    return x + 0.5 * jnp.dot(act, wd_ref[0], preferred_element_type=F32)


def _ffn_kernel(x_ref, g_ref, wg_ref, wu_ref, wd_ref, o_ref):
    o_ref[...] = _ffn_value(x_ref[...], g_ref, wg_ref, wu_ref, wd_ref)


def _layer_weights(layer, w):
    return pl.BlockSpec((1,) + w.shape[1:], lambda *_: (layer, 0, 0), pipeline_mode=pl.Buffered(1))


def half_ffn(x, g, wg, wu, wd, layer):
    m, d = x.shape
    tm = _row_tile(m)
    row = pl.BlockSpec((tm, d), lambda i: (i, 0))
    lw = lambda w: _layer_weights(layer, w)
    return pl.pallas_call(
        _ffn_kernel,
        grid=(m // tm,),
        in_specs=[row, _resident((1, d)), lw(wg), lw(wu), lw(wd)],
        out_specs=row,
        out_shape=jax.ShapeDtypeStruct((m, d), F32),
        compiler_params=_params("parallel"),
        name="half_ffn",
    )(x, g[layer].reshape(1, d), wg, wu, wd)


_PRE_OPERANDS = {"none": 0, "proj": 2, "glu": 5}


def _ffn_pages_kernel(n_pages, pre, post, pt_ref, x_ref, g_ref, wg_ref, wu_ref, wd_ref, g2_ref, *refs):
    n_pre = _PRE_OPERANDS[pre]
    pre_refs, pages, outs = refs[:n_pre], refs[n_pre:n_pre + n_pages], refs[n_pre + n_pages:]
    sums_ref = outs[-1]
    per_block = MOBA_BLOCK // PAGE_SIZE
    for r in range(n_pages // per_block):
        total = jnp.sum(pages[r * per_block][0], axis=0)
        for j in range(1, per_block):
            total = total + jnp.sum(pages[r * per_block + j][0], axis=0)
        sums_ref[0, r] = total
    x = x_ref[...]
    if pre == "proj":
        a_ref, w_ref = pre_refs
        x = x + jnp.dot(a_ref[...], w_ref[...], preferred_element_type=F32)
    elif pre == "glu":
        x = _glu_value(pre_refs[0], x, *pre_refs[1:])
    x_new = _ffn_value(x, g_ref, wg_ref, wu_ref, wd_ref)
    if post == "normed":
        outs[0][...] = _rms(x_new, g2_ref[...])
        return
    outs[0][...] = x_new
    if post == "x+planes":
        u = _rms(x_new, g2_ref[...])
        for v in range(outs[1].shape[0]):
            outs[1][v] = u[:, v * LANES:(v + 1) * LANES]


def half_ffn_with_page_sums(x, g, wg, wu, wd, layer, cache_k, page_table, first_seq, n_seqs,
                            pre=("none",), post="x", g_post=None):
    m, d = x.shape
    tm = FFN_PAGES_ROW_TILE if m % FFN_PAGES_ROW_TILE == 0 else m
    steps = m // tm
    n_pool, ps, nh, hd = cache_k.shape
    n_pages = page_table.shape[1]
    per_block = MOBA_BLOCK // ps
    assert ps == PAGE_SIZE and (n_seqs * n_pages) % steps == 0
    per_step = n_seqs * n_pages // steps
    assert per_step % per_block == 0 and n_pages % per_step == 0
    steps_per_seq = n_pages // per_step
    row = pl.BlockSpec((tm, d), lambda i, pt: (i, 0))
    vec = pl.BlockSpec((1, d), lambda i, pt: (0, 0), pipeline_mode=pl.Buffered(1))
    mat = pl.BlockSpec((d, d), lambda i, pt: (0, 0), pipeline_mode=pl.Buffered(1))
    planes = pl.BlockSpec((d // LANES, tm, LANES), lambda i, pt: (0, i, 0))
    lw = lambda w: _layer_weights(layer, w)
    page = lambda r: pl.BlockSpec(
        (1, ps, nh, hd),
        lambda i, pt: (pt[first_seq + i // steps_per_seq, (i % steps_per_seq) * per_step + r], 0, 0, 0))
    kind, *pre_args = pre
    assert len(pre_args) == _PRE_OPERANDS[kind]
    if kind == "proj":
        pre_specs = [row, mat]
    elif kind == "glu":
        pre_specs = [planes, vec, vec, mat, mat]
        pre_args = [pre_args[0], pre_args[1].reshape(1, d), pre_args[2].reshape(1, d), *pre_args[3:]]
    else:
        pre_specs = []
    out_specs, out_shape = [row], [jax.ShapeDtypeStruct((m, d), F32)]
    if post == "x+planes":
        out_specs.append(planes)
        out_shape.append(jax.ShapeDtypeStruct((d // LANES, m, LANES), F32))
    out_specs.append(pl.BlockSpec((1, per_step // per_block, nh, hd),
                                  lambda i, pt: (i // steps_per_seq, i % steps_per_seq, 0, 0)))
    out_shape.append(jax.ShapeDtypeStruct((n_seqs, n_pages // per_block, nh, hd), F32))
    g_post = g[layer] if g_post is None else g_post
    return pl.pallas_call(
        functools.partial(_ffn_pages_kernel, per_step, kind, post),
        grid_spec=pltpu.PrefetchScalarGridSpec(
            num_scalar_prefetch=1,
            grid=(steps,),
            in_specs=[row, vec, lw(wg), lw(wu), lw(wd), vec] + pre_specs + [page(r) for r in range(per_step)],
            out_specs=out_specs,
        ),
        out_shape=out_shape,
        compiler_params=_params("parallel"),
        name="half_ffn_page_sums",
    )(page_table, x, g[layer].reshape(1, d), wg, wu, wd, g_post.reshape(1, d), *pre_args,
      *([cache_k] * per_step))


def _rmsnorm_kernel(x_ref, g_ref, o_ref):
    o_ref[...] = _rms(x_ref[...], g_ref[...]).astype(o_ref.dtype)


def rmsnorm(x, g, dtype):
    m, d = x.shape
    tm = _row_tile(m)
    row = pl.BlockSpec((tm, d), lambda i: (i, 0))
    return pl.pallas_call(
        _rmsnorm_kernel,
        grid=(m // tm,),
        in_specs=[row, _resident((1, d))],
        out_specs=row,
        out_shape=jax.ShapeDtypeStruct((m, d), dtype),
        compiler_params=_params("parallel"),
        name="rmsnorm",
    )(x, g.reshape(1, d))


def _glu_value(y_ref, x, g_ref, d_ref, wa_ref, wb_ref):
    u = _rms(x, g_ref[...])
    y = jnp.concatenate([y_ref[v] for v in range(y_ref.shape[0])], axis=1)
    act = jax.nn.gelu(y + d_ref[...] * u).astype(BF16)
    a = jnp.dot(act, wa_ref[...], preferred_element_type=F32)
    b = jnp.dot(act, wb_ref[...], preferred_element_type=F32)
    return x + a * jax.nn.sigmoid(b)


def _glu_kernel(y_ref, x_ref, g_ref, d_ref, wa_ref, wb_ref, o_ref):
    o_ref[...] = _glu_value(y_ref, x_ref[...], g_ref, d_ref, wa_ref, wb_ref)


def ssm_glu(y, x, g_mix, d_skip, wa, wb):
    m, d = x.shape
    tm = _row_tile(m)
    row = pl.BlockSpec((tm, d), lambda i: (i, 0))
    planes = pl.BlockSpec((d // LANES, tm, LANES), lambda i: (0, i, 0))
    return pl.pallas_call(
        _glu_kernel,
        grid=(m // tm,),
        in_specs=[planes, row, _resident((1, d)), _resident((1, d)), _resident((d, d)), _resident((d, d))],
        out_specs=row,
        out_shape=jax.ShapeDtypeStruct((m, d), F32),
        compiler_params=_params("parallel"),
        name="ssm_glu",
    )(y, x, g_mix.reshape(1, d), d_skip.reshape(1, d), wa, wb)


def _kv_kernel(n_sum, x_ref, g_ref, wk_ref, wv_ref, k_ref, v_ref, kb_ref, vb_ref, *sum_ref):
    h = _rms(x_ref[...], g_ref[...]).astype(BF16)
    k = jnp.dot(h, wk_ref[...], preferred_element_type=F32)
    v = jnp.dot(h, wv_ref[...], preferred_element_type=F32)
    k_ref[...] = k
    v_ref[...] = v
    kb_ref[...] = k.astype(BF16)
    vb_ref[...] = v.astype(BF16)
    for r in range(n_sum):
        sum_ref[0][r] = jnp.sum(k[r * MOBA_BLOCK:(r + 1) * MOBA_BLOCK], axis=0, keepdims=True)


def shared_kv(x, g, wk, wv, with_block_sums):
    m, d = x.shape
    tm = _row_tile(m)
    n_sum = tm // MOBA_BLOCK if with_block_sums else 0
    row = pl.BlockSpec((tm, d), lambda i: (i, 0))
    out_specs = [row, row, row, row]
    out_shape = [jax.ShapeDtypeStruct((m, d), F32)] * 2 + [jax.ShapeDtypeStruct((m, d), BF16)] * 2
    if with_block_sums:
        assert tm % MOBA_BLOCK == 0
        out_specs.append(pl.BlockSpec((n_sum, 1, d), lambda i: (i, 0, 0)))
        out_shape.append(jax.ShapeDtypeStruct((m // MOBA_BLOCK, 1, d), F32))
    return pl.pallas_call(
        functools.partial(_kv_kernel, n_sum),
        grid=(m // tm,),
        in_specs=[row, _resident((1, d)), _resident((d, d)), _resident((d, d))],
        out_specs=out_specs,
        out_shape=out_shape,
        compiler_params=_params("parallel"),
        name="shared_kv",
    )(x, g.reshape(1, d), wk, wv)


def _proj_residual_kernel(a_ref, w_ref, x_ref, o_ref):
    o_ref[...] = x_ref[...] + jnp.dot(a_ref[...], w_ref[...], preferred_element_type=F32)


def proj_residual(a, w, x):
    m, d = x.shape
    tm = _row_tile(m)
    row = pl.BlockSpec((tm, d), lambda i: (i, 0))
    return pl.pallas_call(
        _proj_residual_kernel,
        grid=(m // tm,),
        in_specs=[row, _resident((d, d)), row],
        out_specs=row,
        out_shape=jax.ShapeDtypeStruct((m, d), F32),
        compiler_params=_params("parallel"),
        name="proj_residual",
    )(a, w, x)


def _norm_proj_kernel(x_ref, g_ref, w_ref, o_ref):
    h = _rms(x_ref[...], g_ref[...]).astype(BF16)
    o_ref[...] = jnp.dot(h, w_ref[...], preferred_element_type=F32)


def norm_proj(x, g, w):
    m, d = x.shape
    tm = _row_tile(m)
    row = pl.BlockSpec((tm, d), lambda i: (i, 0))
    return pl.pallas_call(
        _norm_proj_kernel,
        grid=(m // tm,),
        in_specs=[row, _resident((1, d)), _resident((d, d))],
        out_specs=row,
        out_shape=jax.ShapeDtypeStruct((m, d), F32),
        compiler_params=_params("parallel"),
        name="norm_proj",
    )(x, g.reshape(1, d), w)


def _ssm_weights_kernel(are_ref, aim_ref, ls_ref, bre_ref, bim_ref, cre_ref, cim_ref,
                        opw_ref, bst_ref, cst_ref, lb_ref, m0_ref, b0_ref, c1_ref):
    p, gs, t_len = STATE_DIM, GROUP_SIZE, SSM_CHUNK
    zeros = lambda r, c: jnp.zeros((r, c), F32)
    for a in range(GROUPS_PER_BLOCK):
        lam_re = jnp.minimum(are_ref[a], -1e-4)
        lam_im = aim_ref[a]
        dt = jnp.exp(ls_ref[a])
        lag = lax.broadcasted_iota(jnp.int32, (t_len + 8, p), 0).astype(F32)
        mag = jnp.exp(lag * (lam_re * dt))
        ang = lag * (lam_im * dt)
        pw_re = mag * jnp.cos(ang)
        pw_im = mag * jnp.sin(ang)
        lb_re = pw_re[1:2]
        lb_im = pw_im[1:2]
        den = lam_re * lam_re + lam_im * lam_im
        nr = lb_re - 1.0
        coef_re = (nr * lam_re + lb_im * lam_im) / den
        coef_im = (lb_im * lam_re - nr * lam_im) / den
        bt_re = bre_ref[a]
        bt_im = bim_ref[a]
        bb_re = coef_re * bt_re - coef_im * bt_im
        bb_im = coef_re * bt_im + coef_im * bt_re
        bb_cat = jnp.concatenate([bb_re, bb_im], axis=1)
        c_re = cre_ref[a]
        c_im = cim_ref[a]

        def c_pow(t):
            re = c_re * pw_re[t:t + 1] - c_im * pw_im[t:t + 1]
            im = c_re * pw_im[t:t + 1] + c_im * pw_re[t:t + 1]
            return jnp.concatenate([re, -im], axis=1)

        def in_block_rows(x):
            parts = ([zeros(a * gs, x.shape[1])] if a else []) + [x]
            rest = LANES - (a + 1) * gs
            return jnp.concatenate(parts + ([zeros(rest, x.shape[1])] if rest else []), axis=0)

        def in_block_cols(x):
            parts = ([zeros(gs, a * 2 * p)] if a else []) + [x]
            rest = BLOCK_STATE - (a + 1) * 2 * p
            return jnp.concatenate(parts + ([zeros(gs, rest)] if rest else []), axis=1)

        cp = [c_pow(t) for t in range(t_len + 1)]
        spread = jnp.concatenate([in_block_rows(cp[t]) for t in range(t_len)], axis=0)
        mt = lax.dot_general(bb_cat, spread, _NT, precision=lax.Precision.HIGHEST,
                             preferred_element_type=F32)
        for s in range(t_len):
            rows = slice(s * LANES + a * gs, s * LANES + (a + 1) * gs)
            shifted = mt if s == 0 else jnp.concatenate(
                [zeros(gs, s * LANES), mt[:, :(t_len - s) * LANES]], axis=1)
            opw_ref[0, rows, :] = shifted.astype(opw_ref.dtype)
            q = t_len - 1 - s
            re = bb_re * pw_re[q:q + 1] - bb_im * pw_im[q:q + 1]
            im = bb_re * pw_im[q:q + 1] + bb_im * pw_re[q:q + 1]
            bst_ref[0, rows, :] = in_block_cols(jnp.concatenate([re, im], axis=1)).astype(bst_ref.dtype)
            cst_ref[0, rows, :] = in_block_cols(cp[s + 1]).astype(cst_ref.dtype)
        lb_ref[a, 0:1, :] = jnp.concatenate([pw_re[t_len:t_len + 1], pw_im[t_len:t_len + 1]], axis=1)
        lb_ref[a, 1:2, :] = jnp.concatenate([lb_re, lb_im], axis=1)
        lb_ref[a, 2:8, :] = zeros(6, 2 * p)
        m0_ref[a] = mt[:, :LANES]
        b0_ref[a] = bb_cat
        c1_ref[a] = cp[1]


def ssm_weights(a_re, a_im, log_step, b_re, b_im, c_re, c_im):
    g, p, gs, gb = N_GROUPS, STATE_DIM, GROUP_SIZE, GROUPS_PER_BLOCK
    tl = SSM_CHUNK * LANES
    grp = lambda r, c: pl.BlockSpec((gb, r, c), lambda i: (i, 0, 0))
    blk = lambda r, c: pl.BlockSpec((1, r, c), lambda i: (i, 0, 0))
    return pl.pallas_call(
        _ssm_weights_kernel,
        grid=(g // gb,),
        in_specs=[grp(1, p), grp(1, p), grp(1, 1), grp(gs, p), grp(gs, p), grp(gs, p), grp(gs, p)],
        out_specs=[blk(tl, tl), blk(tl, BLOCK_STATE), blk(tl, BLOCK_STATE),
                   grp(8, 2 * p), grp(gs, LANES), grp(gs, 2 * p), grp(gs, 2 * p)],
        out_shape=[jax.ShapeDtypeStruct((g // gb, tl, tl), BF16),
                   jax.ShapeDtypeStruct((g // gb, tl, BLOCK_STATE), BF16),
                   jax.ShapeDtypeStruct((g // gb, tl, BLOCK_STATE), BF16),
                   jax.ShapeDtypeStruct((g, 8, 2 * p), F32),
                   jax.ShapeDtypeStruct((g, gs, LANES), F32),
                   jax.ShapeDtypeStruct((g, gs, 2 * p), F32),
                   jax.ShapeDtypeStruct((g, gs, 2 * p), F32)],
        compiler_params=_params("parallel"),
        name="ssm_weights",
    )(a_re.reshape(g, 1, p), a_im.reshape(g, 1, p), log_step.reshape(g, 1, 1),
      b_re.transpose(0, 2, 1), b_im.transpose(0, 2, 1), c_re, c_im)


def _lane_block(u_ref, n_chunks):
    return jnp.concatenate(
        [u_ref[0, pl.ds(t, n_chunks, stride=SSM_CHUNK), :] for t in range(SSM_CHUNK)], axis=1).astype(BF16)


def _chunk_state_kernel(n_chunks, u_ref, bst_ref, e_ref):
    e_ref[0] = jnp.dot(_lane_block(u_ref, n_chunks), bst_ref[0], preferred_element_type=F32)


def _chunk_tile(nc):
    return SSM_CHUNK_TILE if nc % SSM_CHUNK_TILE == 0 else nc


def chunk_states(u, bst):
    nv, m, _ = u.shape
    nc = m // SSM_CHUNK
    nct = _chunk_tile(nc)
    return pl.pallas_call(
        functools.partial(_chunk_state_kernel, nct),
        grid=(nv, nc // nct),
        in_specs=[pl.BlockSpec((1, nct * SSM_CHUNK, LANES), lambda v, i: (v, i, 0)),
                  pl.BlockSpec((1,) + bst.shape[1:], lambda v, i: (v, 0, 0))],
        out_specs=pl.BlockSpec((1, nct, BLOCK_STATE), lambda v, i: (v, i, 0)),
        out_shape=jax.ShapeDtypeStruct((nv, nc, BLOCK_STATE), F32),
        compiler_params=_params("parallel", "parallel"),
        name="ssm_chunk_states",
    )(u, bst)


def _swap_halves(x):
    half = STATE_DIM
    n = x.shape[-1]
    lane = lax.broadcasted_iota(jnp.int32, x.shape, x.ndim - 1)
    return jnp.where(lane % (2 * half) < half, pltpu.roll(x, n - half, axis=x.ndim - 1),
                     pltpu.roll(x, half, axis=x.ndim - 1))


def _scan_kernel(n_steps, e_ref, h0_ref, a_ref, hin_ref, hfin_ref, h_scr, hs_scr, e_scr, hin_scr):
    @pl.when(pl.program_id(1) == 0)
    def _():
        h_scr[...] = h0_ref[0]
        hs_scr[...] = _swap_halves(h0_ref[0])

    e_scr[...] = jnp.swapaxes(e_ref[...], 0, 1)
    a_same = a_ref[0]
    a_cross = a_ref[1]
    a_cross_s = a_ref[2]

    def body(t, carry):
        h, hs = carry
        e = e_scr[t]
        es = _swap_halves(e)
        hin_scr[t] = h
        return a_same * h + a_cross * hs + e, a_same * hs + a_cross_s * h + es

    h, hs = lax.fori_loop(0, n_steps, body, (h_scr[...], hs_scr[...]), unroll=8)
    h_scr[...] = h
    hs_scr[...] = hs
    hfin_ref[0] = h
    hin_ref[...] = jnp.swapaxes(hin_scr[...], 0, 1).astype(hin_ref.dtype)


def chunk_scan(e, h0, a):
    nv, nc, w = e.shape
    n_seq = h0.shape[0]
    per_seq = nc // n_seq
    ct = SCAN_TILE if per_seq % SCAN_TILE == 0 else per_seq
    n_ct = per_seq // ct
    tile = pl.BlockSpec((nv, ct, w), lambda b, c: (0, b * n_ct + c, 0))
    seq = pl.BlockSpec((1, nv, w), lambda b, c: (b, 0, 0))
    return pl.pallas_call(
        functools.partial(_scan_kernel, ct),
        grid=(n_seq, n_ct),
        in_specs=[tile, seq, _resident((3, nv, w))],
        out_specs=[tile, seq],
        out_shape=[jax.ShapeDtypeStruct((nv, nc, w), BF16), jax.ShapeDtypeStruct((n_seq, nv, w), F32)],
        scratch_shapes=[pltpu.VMEM((nv, w), F32), pltpu.VMEM((nv, w), F32),
                        pltpu.VMEM((ct, nv, w), F32), pltpu.VMEM((ct, nv, w), F32)],
        compiler_params=_params("parallel", "arbitrary"),
        name="ssm_chunk_scan",
    )(e, h0, a)


def _chunk_out_kernel(n_chunks, u_ref, hin_ref, opw_ref, cst_ref, y_ref):
    w = _lane_block(u_ref, n_chunks)
    hin = hin_ref[0]
    tile = 2 * LANES
    for c in range(SSM_CHUNK * LANES // tile):
        cols = slice(c * tile, (c + 1) * tile)
        k = (c + 1) * tile
        y = jnp.dot(w[:, :k], opw_ref[0, :k, cols], preferred_element_type=F32)
        y = y + lax.dot_general(hin, cst_ref[0, cols, :], _NT, preferred_element_type=F32)
        for j in range(tile // LANES):
            t = c * (tile // LANES) + j
            y_ref[0, pl.ds(t, n_chunks, stride=SSM_CHUNK), :] = y[:, j * LANES:(j + 1) * LANES]


def chunk_outputs(u, hin, opw, cst):
    m = u.shape[1]
    nv, nc, w = hin.shape
    nct = _chunk_tile(nc)
    rows = nct * SSM_CHUNK
    per_block = lambda a: pl.BlockSpec((1,) + a.shape[1:], lambda v, i: (v, 0, 0))
    return pl.pallas_call(
        functools.partial(_chunk_out_kernel, nct),
        grid=(nv, nc // nct),
        in_specs=[pl.BlockSpec((1, rows, LANES), lambda v, i: (v, i, 0)),
                  pl.BlockSpec((1, nct, w), lambda v, i: (v, i, 0)), per_block(opw), per_block(cst)],
        out_specs=pl.BlockSpec((1, rows, LANES), lambda v, i: (v, i, 0)),
        out_shape=jax.ShapeDtypeStruct((nv, m, LANES), F32),
        compiler_params=_params("parallel", "parallel"),
        name="ssm_chunk_outputs",
    )(u, hin, opw, cst)


def s5_prompt(u, n_seq, opw, bst, cst, lb):
    g, p = N_GROUPS, STATE_DIM
    nv = g // GROUPS_PER_BLOCK
    e = chunk_states(u, bst)
    lt_re, lt_im = lb[:, 0, :p], lb[:, 0, p:]
    per_block = lambda *halves: jnp.concatenate(halves, axis=1).reshape(nv, BLOCK_STATE)
    a = jnp.stack([per_block(lt_re, lt_re), per_block(-lt_im, lt_im), per_block(lt_im, -lt_im)])
    hin, hfin = chunk_scan(e, jnp.zeros((n_seq, nv, BLOCK_STATE), F32), a)
    y = chunk_outputs(u, hin, opw, cst)
    hfin = hfin.reshape(n_seq, g, 2 * p)
    return y, hfin[:, :, :p], hfin[:, :, p:]


def _ssm_step_kernel(u_ref, hre_ref, him_ref, m0_ref, bre_ref, bim_ref, cre_ref, cim_ref,
                     lre_ref, lim_ref, y_ref, ore_ref, oim_ref):
    u = u_ref[...]
    h_re, h_im = hre_ref[...], him_ref[...]
    l_re, l_im = lre_ref[...], lim_ref[...]
    bmm = lambda a, b: jnp.einsum("gbk,gkn->gbn", a, b, preferred_element_type=F32)
    bmm_nt = lambda a, b: jnp.einsum("gbk,gnk->gbn", a, b, preferred_element_type=F32)
    ore_ref[...] = l_re * h_re - l_im * h_im + bmm(u, bre_ref[...])
    oim_ref[...] = l_re * h_im + l_im * h_re + bmm(u, bim_ref[...])
    y_ref[...] = bmm(u, m0_ref[...]) + bmm_nt(h_re, cre_ref[...]) + bmm_nt(h_im, cim_ref[...])


def s5_step(u, h_re, h_im, lb, m0, b0, c1):
    b, d = u.shape
    g, gs, p = N_GROUPS, GROUP_SIZE, STATE_DIM
    m0t = m0.reshape(g, gs, GROUPS_PER_BLOCK, gs).sum(axis=2)
    y, o_re, o_im = pl.pallas_call(
        _ssm_step_kernel,
        out_shape=[jax.ShapeDtypeStruct((g, b, gs), F32),
                   jax.ShapeDtypeStruct((g, b, p), F32), jax.ShapeDtypeStruct((g, b, p), F32)],
        compiler_params=pltpu.CompilerParams(vmem_limit_bytes=VMEM_LIMIT),
        name="ssm_step",
    )(u.reshape(b, g, gs).transpose(1, 0, 2), h_re.transpose(1, 0, 2), h_im.transpose(1, 0, 2),
      m0t, b0[:, :, :p], b0[:, :, p:], c1[:, :, :p], c1[:, :, p:], lb[:, 1:2, :p], lb[:, 1:2, p:])
    y = y.reshape(d // LANES, GROUPS_PER_BLOCK, b, gs).transpose(0, 2, 1, 3).reshape(d // LANES, b, LANES)
    return y, o_re.transpose(1, 0, 2), o_im.transpose(1, 0, 2)


def _select_top_blocks(gate, blk):
    bias = jnp.full(gate.shape, MASK_VALUE, F32)
    for _ in range(TOP_K):
        m = jnp.max(gate, axis=0, keepdims=True)
        first = jnp.min(jnp.where(gate == m, blk, float(gate.shape[0])), axis=0, keepdims=True)
        hit = blk == first
        bias = jnp.where(hit, jnp.where(m > -jnp.inf, 0.0, bias), bias)
        gate = jnp.where(hit, -jnp.inf, gate)
    return bias


def _query_select_kernel(x_ref, g_ref, wq_ref, ks_ref, o_ref):
    own = pl.program_id(1)
    h = _rms(x_ref[...], g_ref[...]).astype(BF16)
    q = jnp.dot(h, wq_ref[...], preferred_element_type=F32)
    qb = q.astype(BF16)
    means = (ks_ref[0] * (1.0 / MOBA_BLOCK)).astype(BF16)
    nb8, tq = means.shape[0], q.shape[0]
    blk = lax.broadcasted_iota(jnp.int32, (nb8, tq), 0)
    past = blk < own
    blk = blk.astype(F32)
    unused = jnp.full((LANES - nb8, tq), MASK_VALUE, F32)
    for hh in range(N_HEADS):
        cols = slice(hh * HEAD_DIM, (hh + 1) * HEAD_DIM)
        gate = lax.dot_general(means[:, cols], qb[:, cols], _NT, preferred_element_type=F32)
        bias = _select_top_blocks(jnp.where(past, gate, -jnp.inf), blk)
        o_ref[0, hh, :, :HEAD_DIM] = (q[:, cols] * (ATTN_SCALE * LOG2_E)).astype(BF16)
        o_ref[0, hh, :, HEAD_DIM:] = jnp.concatenate([bias, unused], axis=0).T.astype(BF16)


def query_select(x, g, wq, block_sums, n_seq):
    m, d = x.shape
    seq = m // n_seq
    nb = seq // MOBA_BLOCK
    nb8 = -(-nb // 8) * 8
    assert seq % MOBA_BLOCK == 0 and nb8 <= LANES
    sums = jnp.pad(block_sums.reshape(n_seq, nb, d), ((0, 0), (0, nb8 - nb), (0, 0)))
    return pl.pallas_call(
        _query_select_kernel,
        grid=(n_seq, nb),
        in_specs=[pl.BlockSpec((MOBA_BLOCK, d), lambda b, i: (b * nb + i, 0)),
                  _resident((1, d)), _resident((d, d)),
                  pl.BlockSpec((1, nb8, d), lambda b, i: (b, 0, 0))],
        out_specs=pl.BlockSpec((1, N_HEADS, MOBA_BLOCK, 2 * HEAD_DIM), lambda b, i: (b, 0, i, 0)),
        out_shape=jax.ShapeDtypeStruct((n_seq, N_HEADS, seq, 2 * HEAD_DIM), BF16),
        compiler_params=_params("parallel", "parallel"),
        name="moba_query_select",
    )(x, g.reshape(1, d), wq, sums)


def _moba_kernel(q_ref, k_ref, v_ref, o_ref):
    i = pl.program_id(2)
    tq = MOBA_BLOCK
    span = MOBA_BLOCKS_PER_STEP
    n_heads = q_ref.shape[1]
    cols = [slice(h * HEAD_DIM, (h + 1) * HEAD_DIM) for h in range(n_heads)]
    own = pl.multiple_of(i * tq, tq)
    row = lax.broadcasted_iota(jnp.int32, (tq, tq), 0)
    col = lax.broadcasted_iota(jnp.int32, (tq, tq), 1)
    causal = col <= row
    ones = jnp.ones((span * tq, HEAD_DIM), BF16)
    state = []
    for h in range(n_heads):
        s = lax.dot_general(q_ref[0, h, :, :HEAD_DIM], k_ref[0, pl.ds(own, tq), cols[h]], _NT,
                            preferred_element_type=F32)
        s = jnp.where(causal, s, MASK_VALUE)
        m = jnp.max(s, axis=1, keepdims=True)
        p = jnp.exp2(s - m)
        v_own = jnp.concatenate([v_ref[0, pl.ds(own, tq), cols[h]], ones[:tq]], axis=1)
        state += [m, jnp.dot(p.astype(BF16), v_own, preferred_element_type=F32)]
    lane = lax.broadcasted_iota(jnp.int32, (span * tq, HEAD_DIM), 1)
    block = lax.broadcasted_iota(jnp.int32, (span * tq, HEAD_DIM), 0) // tq

    def body(g, carry):
        off = pl.multiple_of(g * (span * tq), span * tq)
        one_hot = jnp.where(lane == g * span + block, 1.0, 0.0).astype(BF16)
        out = []
        for h in range(n_heads):
            m, acc = carry[2 * h:2 * h + 2]
            k_ext = jnp.concatenate([k_ref[0, pl.ds(off, span * tq), cols[h]], one_hot], axis=1)
            s = lax.dot_general(q_ref[0, h], k_ext, _NT, preferred_element_type=F32)
            m_new = jnp.maximum(m, jnp.max(s, axis=1, keepdims=True))
            p = jnp.exp2(s - m_new)
            v_ext = jnp.concatenate([v_ref[0, pl.ds(off, span * tq), cols[h]], ones], axis=1)
            acc = jnp.exp2(m - m_new) * acc + jnp.dot(p.astype(BF16), v_ext, preferred_element_type=F32)
            out += [m_new, acc]
        return tuple(out)

    state = lax.fori_loop(0, (i + span - 1) // span, body, tuple(state))
    for h in range(n_heads):
        acc = state[2 * h + 1]
        o_ref[0, :, cols[h]] = (acc[:, :HEAD_DIM] / acc[:, HEAD_DIM:]).astype(o_ref.dtype)


def moba_prompt(q_ext, k_bf, v_bf):
    n_seq, _, seq, _ = q_ext.shape
    d = k_bf.shape[-1]
    nb = seq // MOBA_BLOCK
    hb = MOBA_HEADS_PER_STEP
    assert seq % (MOBA_BLOCKS_PER_STEP * MOBA_BLOCK) == 0
    kv = pl.BlockSpec((1, seq, hb * HEAD_DIM), lambda b, h, i: (b, 0, h), pipeline_mode=pl.Buffered(1))
    out = pl.pallas_call(
        _moba_kernel,
        grid=(n_seq, N_HEADS // hb, nb),
        in_specs=[pl.BlockSpec((1, hb, MOBA_BLOCK, 2 * HEAD_DIM), lambda b, h, i: (b, h, i, 0)), kv, kv],
        out_specs=pl.BlockSpec((1, MOBA_BLOCK, hb * HEAD_DIM), lambda b, h, i: (b, i, h)),
        out_shape=jax.ShapeDtypeStruct((n_seq, seq, d), BF16),
        compiler_params=_params("parallel", "parallel", "arbitrary"),
        name="moba_prompt",
    )(q_ext, k_bf, v_bf)
    return out.reshape(n_seq * seq, d)


def _sample_select_kernel(q_ref, sums_ref, o_ref):
    for s in range(q_ref.shape[0]):
        prod = q_ref[s] * (sums_ref[s] * (1.0 / MOBA_BLOCK))
        gate = jnp.sum(prod, axis=2, keepdims=True)
        nb = gate.shape[0]
        blk = lax.broadcasted_iota(jnp.int32, gate.shape, 0).astype(F32)
        for j in range(TOP_K):
            m = jnp.max(gate, axis=0, keepdims=True)
            first = jnp.min(jnp.where(gate == m, blk, float(nb - 1)), axis=0, keepdims=True)
            o_ref[s, j] = jnp.broadcast_to(first[0], o_ref.shape[2:]).astype(jnp.int32)
            gate = jnp.where(blk == first, -jnp.inf, gate)


def sample_select(q, sums):
    b, nh, hd = q.shape
    nb = sums.shape[1]
    assert nb >= TOP_K
    grp = SELECT_SEQS_PER_STEP if b % SELECT_SEQS_PER_STEP == 0 else b
    out = pl.pallas_call(
        _sample_select_kernel,
        grid=(b // grp,),
        in_specs=[pl.BlockSpec((grp, nh, hd), lambda s: (s, 0, 0)),
                  pl.BlockSpec((grp, nb, nh, hd), lambda s: (s, 0, 0, 0))],
        out_specs=pl.BlockSpec((grp, TOP_K, nh, hd), lambda s: (s, 0, 0, 0)),
        out_shape=jax.ShapeDtypeStruct((b, TOP_K, nh, hd), jnp.int32),
        compiler_params=_params("parallel"),
        name="sample_select",
    )(q, sums)
    return out[:, :, :, 0].transpose(0, 2, 1)


def _sample_attn_kernel(n_pages, sel_ref, pt_ref, q_ref, kn_ref, vn_ref, ck_ref, cv_ref, o_ref,
                        kbuf, vbuf, sem):
    step = pl.program_id(0)
    per_block = MOBA_BLOCK // PAGE_SIZE

    def copies(seq, slot):
        out = []
        for h in range(N_HEADS):
            for j in range(n_pages):
                page = pt_ref[seq, sel_ref[seq, h, j // per_block] * per_block + j % per_block]
                out.append(pltpu.make_async_copy(ck_ref.at[page, :, h, :], kbuf.at[slot, h, j], sem.at[slot]))
                out.append(pltpu.make_async_copy(cv_ref.at[page, :, h, :], vbuf.at[slot, h, j], sem.at[slot]))
        return out

    slot = step % 2

    @pl.when(step == 0)
    def _():
        for c in copies(0, 0):
            c.start()

    @pl.when(step + 1 < pl.num_programs(0))
    def _():
        for c in copies(step + 1, 1 - slot):
            c.start()

    for c in copies(step, slot):
        c.wait()

    q = q_ref[0] * ATTN_SCALE
    rows = []
    for h in range(N_HEADS):
        qh = q[h:h + 1]
        s_new = jnp.sum(qh * kn_ref[0, h:h + 1], axis=1, keepdims=True)
        scores = [jnp.sum(kbuf[slot, h, j] * qh, axis=1, keepdims=True) for j in range(n_pages)]
        m = s_new
        for s in scores:
            m = jnp.maximum(m, jnp.max(s, axis=0, keepdims=True))
        p_new = jnp.exp(s_new - m)
        l = p_new
        acc = p_new * vn_ref[0, h:h + 1]
        for j, s in enumerate(scores):
            p = jnp.exp(s - m)
            l = l + jnp.sum(p, axis=0, keepdims=True)
            acc = acc + jnp.sum(p * vbuf[slot, h, j], axis=0, keepdims=True)
        rows.append(acc / l)
    o_ref[0] = jnp.concatenate(rows, axis=0)


def moba_sample(q, k_new, v_new, cache_k, cache_v, page_table, sel):
    b, nh, hd = q.shape
    ps = cache_k.shape[1]
    n_pages = TOP_K * (MOBA_BLOCK // ps)
    seq = pl.BlockSpec((1, nh, hd), lambda s, sel_ref, pt_ref: (s, 0, 0))
    hbm = pl.BlockSpec(memory_space=pl.ANY)
    return pl.pallas_call(
        functools.partial(_sample_attn_kernel, n_pages),
        grid_spec=pltpu.PrefetchScalarGridSpec(
            num_scalar_prefetch=2,
            grid=(b,),
            in_specs=[seq, seq, seq, hbm, hbm],
            out_specs=seq,
            scratch_shapes=[pltpu.VMEM((2, nh, n_pages, ps, hd), F32),
                            pltpu.VMEM((2, nh, n_pages, ps, hd), F32),
                            pltpu.SemaphoreType.DMA((2,))],
        ),
        out_shape=jax.ShapeDtypeStruct((b, nh, hd), F32),
        compiler_params=_params("arbitrary"),
        name="moba_sample",
    )(sel, page_table, q, k_new, v_new, cache_k, cache_v)


def kernel(x_prompt, x_sample, state_ssm_re, state_ssm_im, cache_k, cache_v, page_table,
           norm_ffn1, w_ffn1_gate, w_ffn1_up, w_ffn1_down, norm_mix,
           norm_ffn2, w_ffn2_gate, w_ffn2_up, w_ffn2_down,
           ssm_a_re, ssm_a_im, ssm_log_step, ssm_b_re, ssm_b_im, ssm_c_re, ssm_c_im,
           ssm_d, ssm_w_glu_a, ssm_w_glu_b,
           norm_kv, w_k, w_v, attn_w_q, attn_w_o, norm_final):
    n_seq, seq, d = x_prompt.shape
    n_dec, dec_seq, _ = x_sample.shape
    assert dec_seq == 1 and norm_ffn1.shape[0] == 2 and ssm_a_re.shape[0] == 1
    n_pool, page_size, n_heads, head_dim = cache_k.shape
    assert (page_table.shape[1] * page_size) % MOBA_BLOCK == 0

    bf = lambda w: w.astype(BF16)
    ffn1 = (norm_ffn1, bf(w_ffn1_gate), bf(w_ffn1_up), bf(w_ffn1_down))
    ffn2 = (norm_ffn2, bf(w_ffn2_gate), bf(w_ffn2_up), bf(w_ffn2_down))
    glu_a, glu_b = bf(ssm_w_glu_a[0]), bf(ssm_w_glu_b[0])
    wk, wv, wq, wo = bf(w_k), bf(w_v), bf(attn_w_q[0]), bf(attn_w_o[0])
    opw, bst, cst, lb, m0, b0, c1 = ssm_weights(ssm_a_re[0], ssm_a_im[0], ssm_log_step[0], ssm_b_re[0],
                                                ssm_b_im[0], ssm_c_re[0], ssm_c_im[0])

    n_host = 4
    assert n_dec % n_host == 0
    block_sums = []

    def prompt_ffn(x, w, layer, **post):
        first = len(block_sums) * (n_dec // n_host)
        *out, sums = half_ffn_with_page_sums(x, *w, layer, cache_k, page_table, first, n_dec // n_host, **post)
        block_sums.append(sums)
        return out[0] if len(out) == 1 else out

    xp = x_prompt.reshape(n_seq * seq, d)
    xp, u = prompt_ffn(xp, ffn1, 0, post="x+planes", g_post=norm_mix[0])
    y, p_re, p_im = s5_prompt(u, n_seq, opw, bst, cst, lb)
    xp = prompt_ffn(xp, ffn2, 0, pre=("glu", y, norm_mix[0], ssm_d[0], glu_a, glu_b))
    k_p, v_p, kb_p, vb_p, ksum_p = shared_kv(xp, norm_kv, wk, wv, True)
    xp = prompt_ffn(xp, ffn1, 1)
    q_ext = query_select(xp, norm_mix[1], wq, ksum_p, n_seq)
    attn_p = moba_prompt(q_ext, kb_p.reshape(n_seq, seq, d), vb_p.reshape(n_seq, seq, d))
    y_prompt = prompt_ffn(xp, ffn2, 1, pre=("proj", attn_p, wo), post="normed",
                          g_post=norm_final).reshape(n_seq, seq, d)

    xs = x_sample.reshape(n_dec, d)
    xs = half_ffn(xs, *ffn1, 0)
    y, s_re, s_im = s5_step(rmsnorm(xs, norm_mix[0], F32), state_ssm_re[0], state_ssm_im[0], lb, m0, b0, c1)
    xs = ssm_glu(y, xs, norm_mix[0], ssm_d[0], glu_a, glu_b)
    xs = half_ffn(xs, *ffn2, 0)
    k_s, v_s, _, _ = shared_kv(xs, norm_kv, wk, wv, False)
    xs = half_ffn(xs, *ffn1, 1)
    q_s = norm_proj(xs, norm_mix[1], wq)
    per_head = lambda a: a.reshape(n_dec, n_heads, head_dim)
    sel = sample_select(per_head(q_s), jnp.concatenate(block_sums, axis=0))
    attn_s = moba_sample(per_head(q_s), per_head(k_s), per_head(v_s), cache_k, cache_v, page_table, sel)
    xs = proj_residual(attn_s.reshape(n_dec, d).astype(BF16), wo, xs)
    xs = half_ffn(xs, *ffn2, 1)
    y_sample = rmsnorm(xs, norm_final, F32).reshape(n_dec, 1, d)

    heads = lambda a, b, s: a.reshape(b, s, n_heads, head_dim)
    return (y_prompt, y_sample, p_re[None], p_im[None], s_re[None], s_im[None],
            heads(k_p, n_seq, seq), heads(v_p, n_seq, seq), heads(k_s, n_dec, 1), heads(v_s, n_dec, 1))
```

```python
import functools

import jax
import jax.numpy as jnp
from jax import lax
from jax.experimental import pallas as pl
from jax.experimental.pallas import tpu as pltpu

F32 = jnp.float32
BF16 = jnp.bfloat16

D_MODEL = 1024
N_GROUPS = 64
GROUP_SIZE = 16
STATE_DIM = 64
N_HEADS = 8
HEAD_DIM = 128
MOBA_BLOCK = 256
TOP_K = 3
PAGE_SIZE = 128
RMS_EPS = 1e-6
ATTN_SCALE = HEAD_DIM ** -0.5
LOG2_E = 1.4426950408889634
MASK_VALUE = -1e30

LANES = 128
SSM_CHUNK = 8
GROUPS_PER_BLOCK = LANES // GROUP_SIZE
BLOCK_STATE = GROUPS_PER_BLOCK * 2 * STATE_DIM
ROW_TILE = 512
FFN_PAGES_ROW_TILE = 256
SSM_CHUNK_TILE = 1024
SCAN_TILE = 128
SELECT_SEQS_PER_STEP = 8
MOBA_BLOCKS_PER_STEP = 4
VMEM_LIMIT = 56 * 1024 * 1024

_NT = (((1,), (1,)), ((), ()))


def _params(*sem):
    return pltpu.CompilerParams(dimension_semantics=sem, vmem_limit_bytes=VMEM_LIMIT)


def _resident(shape):
    zeros = (0,) * len(shape)
    return pl.BlockSpec(shape, lambda *_: zeros, pipeline_mode=pl.Buffered(1))


def _rms(x, g):
    inv = lax.rsqrt(jnp.mean(x * x, axis=-1, keepdims=True) + RMS_EPS)
    return x * inv * g


def _row_tile(m):
    return ROW_TILE if m % ROW_TILE == 0 else m


def _ffn_value(x, g_ref, wg_ref, wu_ref, wd_ref):
    h = _rms(x, g_ref[...]).astype(BF16)
    a = jnp.dot(h, wg_ref[0], preferred_element_type=F32)
    b = jnp.dot(h, wu_ref[0], preferred_element_type=F32)
    act = (jax.nn.silu(a) * b).astype(BF16)
    return x + 0.5 * jnp.dot(act, wd_ref[0], preferred_element_type=F32)


def _ffn_kernel(x_ref, g_ref, wg_ref, wu_ref, wd_ref, o_ref):
    o_ref[...] = _ffn_value(x_ref[...], g_ref, wg_ref, wu_ref, wd_ref)


def _layer_weights(layer, w):
    return pl.BlockSpec((1,) + w.shape[1:], lambda *_: (layer, 0, 0), pipeline_mode=pl.Buffered(1))


def half_ffn(x, g, wg, wu, wd, layer):
    m, d = x.shape
    tm = _row_tile(m)
    row = pl.BlockSpec((tm, d), lambda i: (i, 0))
    lw = lambda w: _layer_weights(layer, w)
    return pl.pallas_call(
        _ffn_kernel,
        grid=(m // tm,),
        in_specs=[row, _resident((1, d)), lw(wg), lw(wu), lw(wd)],
        out_specs=row,
        out_shape=jax.ShapeDtypeStruct((m, d), F32),
        compiler_params=_params("parallel"),
        name="half_ffn",
    )(x, g[layer].reshape(1, d), wg, wu, wd)


_PRE_OPERANDS = {"none": 0, "proj": 2, "glu": 5}


def _ffn_pages_kernel(n_pages, pre, post, pt_ref, x_ref, g_ref, wg_ref, wu_ref, wd_ref, g2_ref, *refs):
    n_pre = _PRE_OPERANDS[pre]
    pre_refs, pages, outs = refs[:n_pre], refs[n_pre:n_pre + n_pages], refs[n_pre + n_pages:]
    sums_ref = outs[-1]
    per_block = MOBA_BLOCK // PAGE_SIZE
    for r in range(n_pages // per_block):
        total = jnp.sum(pages[r * per_block][0], axis=0)
        for j in range(1, per_block):
            total = total + jnp.sum(pages[r * per_block + j][0], axis=0)
        sums_ref[0, r] = total
    x = x_ref[...]
    if pre == "proj":
        a_ref, w_ref = pre_refs
        x = x + jnp.dot(a_ref[...], w_ref[...], preferred_element_type=F32)
    elif pre == "glu":
        x = _glu_value(pre_refs[0], x, *pre_refs[1:])
    x_new = _ffn_value(x, g_ref, wg_ref, wu_ref, wd_ref)
    if post == "normed":
        outs[0][...] = _rms(x_new, g2_ref[...])
        return
    outs[0][...] = x_new
    if post == "x+planes":
        u = _rms(x_new, g2_ref[...])
        for v in range(outs[1].shape[0]):
            outs[1][v] = u[:, v * LANES:(v + 1) * LANES]


def half_ffn_with_page_sums(x, g, wg, wu, wd, layer, cache_k, page_table, first_seq, n_seqs,
                            pre=("none",), post="x", g_post=None):
    m, d = x.shape
    tm = FFN_PAGES_ROW_TILE if m % FFN_PAGES_ROW_TILE == 0 else m
    steps = m // tm
    n_pool, ps, nh, hd = cache_k.shape
    n_pages = page_table.shape[1]
    per_block = MOBA_BLOCK // ps
    assert ps == PAGE_SIZE and (n_seqs * n_pages) % steps == 0
    per_step = n_seqs * n_pages // steps
    assert per_step % per_block == 0 and n_pages % per_step == 0
    steps_per_seq = n_pages // per_step
    row = pl.BlockSpec((tm, d), lambda i, pt: (i, 0))
    vec = pl.BlockSpec((1, d), lambda i, pt: (0, 0), pipeline_mode=pl.Buffered(1))
    mat = pl.BlockSpec((d, d), lambda i, pt: (0, 0), pipeline_mode=pl.Buffered(1))
    planes = pl.BlockSpec((d // LANES, tm, LANES), lambda i, pt: (0, i, 0))
    lw = lambda w: _layer_weights(layer, w)
    page = lambda r: pl.BlockSpec(
        (1, ps, nh, hd),
        lambda i, pt: (pt[first_seq + i // steps_per_seq, (i % steps_per_seq) * per_step + r], 0, 0, 0))
    kind, *pre_args = pre
    assert len(pre_args) == _PRE_OPERANDS[kind]
    if kind == "proj":
        pre_specs = [row, mat]
    elif kind == "glu":
        pre_specs = [planes, vec, vec, mat, mat]
        pre_args = [pre_args[0], pre_args[1].reshape(1, d), pre_args[2].reshape(1, d), *pre_args[3:]]
    else:
        pre_specs = []
    out_specs, out_shape = [row], [jax.ShapeDtypeStruct((m, d), F32)]
    if post == "x+planes":
        out_specs.append(planes)
        out_shape.append(jax.ShapeDtypeStruct((d // LANES, m, LANES), F32))
    out_specs.append(pl.BlockSpec((1, per_step // per_block, nh, hd),
                                  lambda i, pt: (i // steps_per_seq, i % steps_per_seq, 0, 0)))
    out_shape.append(jax.ShapeDtypeStruct((n_seqs, n_pages // per_block, nh, hd), F32))
    g_post = g[layer] if g_post is None else g_post
    return pl.pallas_call(
        functools.partial(_ffn_pages_kernel, per_step, kind, post),
        grid_spec=pltpu.PrefetchScalarGridSpec(
            num_scalar_prefetch=1,
            grid=(steps,),
            in_specs=[row, vec, lw(wg), lw(wu), lw(wd), vec] + pre_specs + [page(r) for r in range(per_step)],
            out_specs=out_specs,
        ),
        out_shape=out_shape,
        compiler_params=_params("parallel"),
        name="half_ffn_page_sums",
    )(page_table, x, g[layer].reshape(1, d), wg, wu, wd, g_post.reshape(1, d), *pre_args,
      *([cache_k] * per_step))


def _rmsnorm_kernel(x_ref, g_ref, o_ref):
    o_ref[...] = _rms(x_ref[...], g_ref[...]).astype(o_ref.dtype)


def rmsnorm(x, g, dtype):
    m, d = x.shape
    tm = _row_tile(m)
    row = pl.BlockSpec((tm, d), lambda i: (i, 0))
    return pl.pallas_call(
        _rmsnorm_kernel,
        grid=(m // tm,),
        in_specs=[row, _resident((1, d))],
        out_specs=row,
        out_shape=jax.ShapeDtypeStruct((m, d), dtype),
        compiler_params=_params("parallel"),
        name="rmsnorm",
    )(x, g.reshape(1, d))


def _glu_value(y_ref, x, g_ref, d_ref, wa_ref, wb_ref):
    u = _rms(x, g_ref[...])
    y = jnp.concatenate([y_ref[v] for v in range(y_ref.shape[0])], axis=1)
    act = jax.nn.gelu(y + d_ref[...] * u).astype(BF16)
    a = jnp.dot(act, wa_ref[...], preferred_element_type=F32)
    b = jnp.dot(act, wb_ref[...], preferred_element_type=F32)
    return x + a * jax.nn.sigmoid(b)


def _glu_kernel(y_ref, x_ref, g_ref, d_ref, wa_ref, wb_ref, o_ref):
    o_ref[...] = _glu_value(y_ref, x_ref[...], g_ref, d_ref, wa_ref, wb_ref)


def ssm_glu(y, x, g_mix, d_skip, wa, wb):
    m, d = x.shape
    tm = _row_tile(m)
    row = pl.BlockSpec((tm, d), lambda i: (i, 0))
    planes = pl.BlockSpec((d // LANES, tm, LANES), lambda i: (0, i, 0))
    return pl.pallas_call(
        _glu_kernel,
        grid=(m // tm,),
        in_specs=[planes, row, _resident((1, d)), _resident((1, d)), _resident((d, d)), _resident((d, d))],
        out_specs=row,
        out_shape=jax.ShapeDtypeStruct((m, d), F32),
        compiler_params=_params("parallel"),
        name="ssm_glu",
    )(y, x, g_mix.reshape(1, d), d_skip.reshape(1, d), wa, wb)


def _kv_kernel(n_sum, x_ref, g_ref, wk_ref, wv_ref, k_ref, v_ref, kb_ref, vb_ref, *sum_ref):
    h = _rms(x_ref[...], g_ref[...]).astype(BF16)
    k = jnp.dot(h, wk_ref[...], preferred_element_type=F32)
    v = jnp.dot(h, wv_ref[...], preferred_element_type=F32)
    k_ref[...] = k
    v_ref[...] = v
    kb_ref[...] = k.astype(BF16)
    vb_ref[...] = v.astype(BF16)
    for r in range(n_sum):
        sum_ref[0][r] = jnp.sum(k[r * MOBA_BLOCK:(r + 1) * MOBA_BLOCK], axis=0, keepdims=True)


def shared_kv(x, g, wk, wv, with_block_sums):
    m, d = x.shape
    tm = _row_tile(m)
    n_sum = tm // MOBA_BLOCK if with_block_sums else 0
    row = pl.BlockSpec((tm, d), lambda i: (i, 0))
    out_specs = [row, row, row, row]
    out_shape = [jax.ShapeDtypeStruct((m, d), F32)] * 2 + [jax.ShapeDtypeStruct((m, d), BF16)] * 2
    if with_block_sums:
        assert tm % MOBA_BLOCK == 0
        out_specs.append(pl.BlockSpec((n_sum, 1, d), lambda i: (i, 0, 0)))
        out_shape.append(jax.ShapeDtypeStruct((m // MOBA_BLOCK, 1, d), F32))
    return pl.pallas_call(
        functools.partial(_kv_kernel, n_sum),
        grid=(m // tm,),
        in_specs=[row, _resident((1, d)), _resident((d, d)), _resident((d, d))],
        out_specs=out_specs,
        out_shape=out_shape,
        compiler_params=_params("parallel"),
        name="shared_kv",
    )(x, g.reshape(1, d), wk, wv)


def _proj_residual_kernel(a_ref, w_ref, x_ref, o_ref):
    o_ref[...] = x_ref[...] + jnp.dot(a_ref[...], w_ref[...], preferred_element_type=F32)


def proj_residual(a, w, x):
    m, d = x.shape
    tm = _row_tile(m)
    row = pl.BlockSpec((tm, d), lambda i: (i, 0))
    return pl.pallas_call(
        _proj_residual_kernel,
        grid=(m // tm,),
        in_specs=[row, _resident((d, d)), row],
        out_specs=row,
        out_shape=jax.ShapeDtypeStruct((m, d), F32),
        compiler_params=_params("parallel"),
        name="proj_residual",
    )(a, w, x)


def _norm_proj_kernel(x_ref, g_ref, w_ref, o_ref):
    h = _rms(x_ref[...], g_ref[...]).astype(BF16)
    o_ref[...] = jnp.dot(h, w_ref[...], preferred_element_type=F32)


def norm_proj(x, g, w):
    m, d = x.shape
    tm = _row_tile(m)
    row = pl.BlockSpec((tm, d), lambda i: (i, 0))
    return pl.pallas_call(
        _norm_proj_kernel,
        grid=(m // tm,),
        in_specs=[row, _resident((1, d)), _resident((d, d))],
        out_specs=row,
        out_shape=jax.ShapeDtypeStruct((m, d), F32),
        compiler_params=_params("parallel"),
        name="norm_proj",
    )(x, g.reshape(1, d), w)


def _ssm_weights_kernel(are_ref, aim_ref, ls_ref, bre_ref, bim_ref, cre_ref, cim_ref,
                        opw_ref, bst_ref, cst_ref, lb_ref, m0_ref, b0_ref, c1_ref):
    p, gs, t_len = STATE_DIM, GROUP_SIZE, SSM_CHUNK
    zeros = lambda r, c: jnp.zeros((r, c), F32)
    for a in range(GROUPS_PER_BLOCK):
        lam_re = jnp.minimum(are_ref[a], -1e-4)
        lam_im = aim_ref[a]
        dt = jnp.exp(ls_ref[a])
        lag = lax.broadcasted_iota(jnp.int32, (t_len + 8, p), 0).astype(F32)
        mag = jnp.exp(lag * (lam_re * dt))
        ang = lag * (lam_im * dt)
        pw_re = mag * jnp.cos(ang)
        pw_im = mag * jnp.sin(ang)
        lb_re = pw_re[1:2]
        lb_im = pw_im[1:2]
        den = lam_re * lam_re + lam_im * lam_im
        nr = lb_re - 1.0
        coef_re = (nr * lam_re + lb_im * lam_im) / den
        coef_im = (lb_im * lam_re - nr * lam_im) / den
        bt_re = bre_ref[a]
        bt_im = bim_ref[a]
        bb_re = coef_re * bt_re - coef_im * bt_im
        bb_im = coef_re * bt_im + coef_im * bt_re
        bb_cat = jnp.concatenate([bb_re, bb_im], axis=1)
        c_re = cre_ref[a]
        c_im = cim_ref[a]

        def c_pow(t):
            re = c_re * pw_re[t:t + 1] - c_im * pw_im[t:t + 1]
            im = c_re * pw_im[t:t + 1] + c_im * pw_re[t:t + 1]
            return jnp.concatenate([re, -im], axis=1)

        def in_block_rows(x):
            parts = ([zeros(a * gs, x.shape[1])] if a else []) + [x]
            rest = LANES - (a + 1) * gs
            return jnp.concatenate(parts + ([zeros(rest, x.shape[1])] if rest else []), axis=0)

        def in_block_cols(x):
            parts = ([zeros(gs, a * 2 * p)] if a else []) + [x]
            rest = BLOCK_STATE - (a + 1) * 2 * p
            return jnp.concatenate(parts + ([zeros(gs, rest)] if rest else []), axis=1)

        cp = [c_pow(t) for t in range(t_len + 1)]
        spread = jnp.concatenate([in_block_rows(cp[t]) for t in range(t_len)], axis=0)
        mt = lax.dot_general(bb_cat, spread, _NT, precision=lax.Precision.HIGHEST,
                             preferred_element_type=F32)
        for s in range(t_len):
            rows = slice(s * LANES + a * gs, s * LANES + (a + 1) * gs)
            shifted = mt if s == 0 else jnp.concatenate(
                [zeros(gs, s * LANES), mt[:, :(t_len - s) * LANES]], axis=1)
            opw_ref[0, rows, :] = shifted.astype(opw_ref.dtype)
            q = t_len - 1 - s
            re = bb_re * pw_re[q:q + 1] - bb_im * pw_im[q:q + 1]
            im = bb_re * pw_im[q:q + 1] + bb_im * pw_re[q:q + 1]
            bst_ref[0, rows, :] = in_block_cols(jnp.concatenate([re, im], axis=1)).astype(bst_ref.dtype)
            cst_ref[0, rows, :] = in_block_cols(cp[s + 1]).astype(cst_ref.dtype)
        lb_ref[a, 0:1, :] = jnp.concatenate([pw_re[t_len:t_len + 1], pw_im[t_len:t_len + 1]], axis=1)
        lb_ref[a, 1:2, :] = jnp.concatenate([lb_re, lb_im], axis=1)
        lb_ref[a, 2:8, :] = zeros(6, 2 * p)
        m0_ref[a] = mt[:, :LANES]
        b0_ref[a] = bb_cat
        c1_ref[a] = cp[1]


def ssm_weights(a_re, a_im, log_step, b_re, b_im, c_re, c_im):
    g, p, gs, gb = N_GROUPS, STATE_DIM, GROUP_SIZE, GROUPS_PER_BLOCK
    tl = SSM_CHUNK * LANES
    grp = lambda r, c: pl.BlockSpec((gb, r, c), lambda i: (i, 0, 0))
    blk = lambda r, c: pl.BlockSpec((1, r, c), lambda i: (i, 0, 0))
    return pl.pallas_call(
        _ssm_weights_kernel,
        grid=(g // gb,),
        in_specs=[grp(1, p), grp(1, p), grp(1, 1), grp(gs, p), grp(gs, p), grp(gs, p), grp(gs, p)],
        out_specs=[blk(tl, tl), blk(tl, BLOCK_STATE), blk(tl, BLOCK_STATE),
                   grp(8, 2 * p), grp(gs, LANES), grp(gs, 2 * p), grp(gs, 2 * p)],
        out_shape=[jax.ShapeDtypeStruct((g // gb, tl, tl), BF16),
                   jax.ShapeDtypeStruct((g // gb, tl, BLOCK_STATE), BF16),
                   jax.ShapeDtypeStruct((g // gb, tl, BLOCK_STATE), BF16),
                   jax.ShapeDtypeStruct((g, 8, 2 * p), F32),
                   jax.ShapeDtypeStruct((g, gs, LANES), F32),
                   jax.ShapeDtypeStruct((g, gs, 2 * p), F32),
                   jax.ShapeDtypeStruct((g, gs, 2 * p), F32)],
        compiler_params=_params("parallel"),
        name="ssm_weights",
    )(a_re.reshape(g, 1, p), a_im.reshape(g, 1, p), log_step.reshape(g, 1, 1),
      b_re.transpose(0, 2, 1), b_im.transpose(0, 2, 1), c_re, c_im)


def _lane_block(u_ref, n_chunks):
    return jnp.concatenate(
        [u_ref[0, pl.ds(t, n_chunks, stride=SSM_CHUNK), :] for t in range(SSM_CHUNK)], axis=1).astype(BF16)


def _chunk_state_kernel(n_chunks, u_ref, bst_ref, e_ref):
    e_ref[0] = jnp.dot(_lane_block(u_ref, n_chunks), bst_ref[0], preferred_element_type=F32)


def _chunk_tile(nc):
    return SSM_CHUNK_TILE if nc % SSM_CHUNK_TILE == 0 else nc


def chunk_states(u, bst):
    nv, m, _ = u.shape
    nc = m // SSM_CHUNK
    nct = _chunk_tile(nc)
    return pl.pallas_call(
        functools.partial(_chunk_state_kernel, nct),
        grid=(nv, nc // nct),
        in_specs=[pl.BlockSpec((1, nct * SSM_CHUNK, LANES), lambda v, i: (v, i, 0)),
                  pl.BlockSpec((1,) + bst.shape[1:], lambda v, i: (v, 0, 0))],
        out_specs=pl.BlockSpec((1, nct, BLOCK_STATE), lambda v, i: (v, i, 0)),
        out_shape=jax.ShapeDtypeStruct((nv, nc, BLOCK_STATE), F32),
        compiler_params=_params("parallel", "parallel"),
        name="ssm_chunk_states",
    )(u, bst)


def _swap_halves(x):
    half = STATE_DIM
    n = x.shape[-1]
    lane = lax.broadcasted_iota(jnp.int32, x.shape, x.ndim - 1)
    return jnp.where(lane % (2 * half) < half, pltpu.roll(x, n - half, axis=x.ndim - 1),
                     pltpu.roll(x, half, axis=x.ndim - 1))


def _scan_kernel(n_steps, e_ref, h0_ref, a_ref, hin_ref, hfin_ref, h_scr, hs_scr, e_scr, hin_scr):
    @pl.when(pl.program_id(1) == 0)
    def _():
        h_scr[...] = h0_ref[0]
        hs_scr[...] = _swap_halves(h0_ref[0])

    e_scr[...] = jnp.swapaxes(e_ref[...], 0, 1)
    a_same = a_ref[0]
    a_cross = a_ref[1]
    a_cross_s = a_ref[2]

    def body(t, carry):
        h, hs = carry
        e = e_scr[t]
        es = _swap_halves(e)
        hin_scr[t] = h
        return a_same * h + a_cross * hs + e, a_same * hs + a_cross_s * h + es

    h, hs = lax.fori_loop(0, n_steps, body, (h_scr[...], hs_scr[...]), unroll=8)
    h_scr[...] = h
    hs_scr[...] = hs
    hfin_ref[0] = h
    hin_ref[...] = jnp.swapaxes(hin_scr[...], 0, 1).astype(hin_ref.dtype)


def chunk_scan(e, h0, a):
    nv, nc, w = e.shape
    n_seq = h0.shape[0]
    per_seq = nc // n_seq
    ct = SCAN_TILE if per_seq % SCAN_TILE == 0 else per_seq
    n_ct = per_seq // ct
    tile = pl.BlockSpec((nv, ct, w), lambda b, c: (0, b * n_ct + c, 0))
    seq = pl.BlockSpec((1, nv, w), lambda b, c: (b, 0, 0))
    return pl.pallas_call(
        functools.partial(_scan_kernel, ct),
        grid=(n_seq, n_ct),
        in_specs=[tile, seq, _resident((3, nv, w))],
        out_specs=[tile, seq],
        out_shape=[jax.ShapeDtypeStruct((nv, nc, w), BF16), jax.ShapeDtypeStruct((n_seq, nv, w), F32)],
        scratch_shapes=[pltpu.VMEM((nv, w), F32), pltpu.VMEM((nv, w), F32),
                        pltpu.VMEM((ct, nv, w), F32), pltpu.VMEM((ct, nv, w), F32)],
        compiler_params=_params("parallel", "arbitrary"),
        name="ssm_chunk_scan",
    )(e, h0, a)


def _chunk_out_kernel(n_chunks, u_ref, hin_ref, opw_ref, cst_ref, y_ref):
    w = _lane_block(u_ref, n_chunks)
    hin = hin_ref[0]
    tile = 2 * LANES
    for c in range(SSM_CHUNK * LANES // tile):
        cols = slice(c * tile, (c + 1) * tile)
        k = (c + 1) * tile
        y = jnp.dot(w[:, :k], opw_ref[0, :k, cols], preferred_element_type=F32)
        y = y + lax.dot_general(hin, cst_ref[0, cols, :], _NT, preferred_element_type=F32)
        for j in range(tile // LANES):
            t = c * (tile // LANES) + j
            y_ref[0, pl.ds(t, n_chunks, stride=SSM_CHUNK), :] = y[:, j * LANES:(j + 1) * LANES]


def chunk_outputs(u, hin, opw, cst):
    m = u.shape[1]
    nv, nc, w = hin.shape
    nct = _chunk_tile(nc)
    rows = nct * SSM_CHUNK
    per_block = lambda a: pl.BlockSpec((1,) + a.shape[1:], lambda v, i: (v, 0, 0))
    return pl.pallas_call(
        functools.partial(_chunk_out_kernel, nct),
        grid=(nv, nc // nct),
        in_specs=[pl.BlockSpec((1, rows, LANES), lambda v, i: (v, i, 0)),
                  pl.BlockSpec((1, nct, w), lambda v, i: (v, i, 0)), per_block(opw), per_block(cst)],
        out_specs=pl.BlockSpec((1, rows, LANES), lambda v, i: (v, i, 0)),
        out_shape=jax.ShapeDtypeStruct((nv, m, LANES), F32),
        compiler_params=_params("parallel", "parallel"),
        name="ssm_chunk_outputs",
    )(u, hin, opw, cst)


def s5_prompt(u, n_seq, opw, bst, cst, lb):
    g, p = N_GROUPS, STATE_DIM
    nv = g // GROUPS_PER_BLOCK
    e = chunk_states(u, bst)
    lt_re, lt_im = lb[:, 0, :p], lb[:, 0, p:]
    per_block = lambda *halves: jnp.concatenate(halves, axis=1).reshape(nv, BLOCK_STATE)
    a = jnp.stack([per_block(lt_re, lt_re), per_block(-lt_im, lt_im), per_block(lt_im, -lt_im)])
    hin, hfin = chunk_scan(e, jnp.zeros((n_seq, nv, BLOCK_STATE), F32), a)
    y = chunk_outputs(u, hin, opw, cst)
    hfin = hfin.reshape(n_seq, g, 2 * p)
    return y, hfin[:, :, :p], hfin[:, :, p:]


def _ssm_step_kernel(u_ref, hre_ref, him_ref, m0_ref, bre_ref, bim_ref, cre_ref, cim_ref,
                     lre_ref, lim_ref, y_ref, ore_ref, oim_ref):
    u = u_ref[...]
    h_re, h_im = hre_ref[...], him_ref[...]
    l_re, l_im = lre_ref[...], lim_ref[...]
    bmm = lambda a, b: jnp.einsum("gbk,gkn->gbn", a, b, preferred_element_type=F32)
    bmm_nt = lambda a, b: jnp.einsum("gbk,gnk->gbn", a, b, preferred_element_type=F32)
    ore_ref[...] = l_re * h_re - l_im * h_im + bmm(u, bre_ref[...])
    oim_ref[...] = l_re * h_im + l_im * h_re + bmm(u, bim_ref[...])
    y_ref[...] = bmm(u, m0_ref[...]) + bmm_nt(h_re, cre_ref[...]) + bmm_nt(h_im, cim_ref[...])


def s5_step(u, h_re, h_im, lb, m0, b0, c1):
    b, d = u.shape
    g, gs, p = N_GROUPS, GROUP_SIZE, STATE_DIM
    m0t = m0.reshape(g, gs, GROUPS_PER_BLOCK, gs).sum(axis=2)
    y, o_re, o_im = pl.pallas_call(
        _ssm_step_kernel,
        out_shape=[jax.ShapeDtypeStruct((g, b, gs), F32),
                   jax.ShapeDtypeStruct((g, b, p), F32), jax.ShapeDtypeStruct((g, b, p), F32)],
        compiler_params=pltpu.CompilerParams(vmem_limit_bytes=VMEM_LIMIT),
        name="ssm_step",
    )(u.reshape(b, g, gs).transpose(1, 0, 2), h_re.transpose(1, 0, 2), h_im.transpose(1, 0, 2),
      m0t, b0[:, :, :p], b0[:, :, p:], c1[:, :, :p], c1[:, :, p:], lb[:, 1:2, :p], lb[:, 1:2, p:])
    y = y.reshape(d // LANES, GROUPS_PER_BLOCK, b, gs).transpose(0, 2, 1, 3).reshape(d // LANES, b, LANES)
    return y, o_re.transpose(1, 0, 2), o_im.transpose(1, 0, 2)


def _select_top_blocks(gate, blk):
    bias = jnp.full(gate.shape, MASK_VALUE, F32)
    for _ in range(TOP_K):
        m = jnp.max(gate, axis=0, keepdims=True)
        first = jnp.min(jnp.where(gate == m, blk, float(gate.shape[0])), axis=0, keepdims=True)
        hit = blk == first
        bias = jnp.where(hit, jnp.where(m > -jnp.inf, 0.0, bias), bias)
        gate = jnp.where(hit, -jnp.inf, gate)
    return bias


def _moba_kernel(x_ref, g_ref, wq_ref, ks_ref, k_ref, v_ref, o_ref, q_scr):
    i = pl.program_id(1)
    tq = MOBA_BLOCK
    span = MOBA_BLOCKS_PER_STEP
    n_heads = q_scr.shape[0]
    cols = [slice(h * HEAD_DIM, (h + 1) * HEAD_DIM) for h in range(n_heads)]

    xn = _rms(x_ref[...], g_ref[...]).astype(BF16)
    q = jnp.dot(xn, wq_ref[...], preferred_element_type=F32)
    qb = q.astype(BF16)
    means = (ks_ref[0] * (1.0 / MOBA_BLOCK)).astype(BF16)
    nb8 = means.shape[0]
    blk = lax.broadcasted_iota(jnp.int32, (nb8, tq), 0)
    past = blk < i
    blk = blk.astype(F32)
    unused = jnp.full((LANES - nb8, tq), MASK_VALUE, F32)
    for h in range(n_heads):
        gate = lax.dot_general(means[:, cols[h]], qb[:, cols[h]], _NT, preferred_element_type=F32)
        bias = _select_top_blocks(jnp.where(past, gate, -jnp.inf), blk)
        q_scr[h, :, :HEAD_DIM] = (q[:, cols[h]] * (ATTN_SCALE * LOG2_E)).astype(BF16)
        q_scr[h, :, HEAD_DIM:] = jnp.concatenate([bias, unused], axis=0).T.astype(BF16)

    own = pl.multiple_of(i * tq, tq)
    row = lax.broadcasted_iota(jnp.int32, (tq, tq), 0)
    col = lax.broadcasted_iota(jnp.int32, (tq, tq), 1)
    causal = col <= row
    ones = jnp.ones((span * tq, HEAD_DIM), BF16)
    state = []
    for h in range(n_heads):
        s = lax.dot_general(q_scr[h, :, :HEAD_DIM], k_ref[0, pl.ds(own, tq), cols[h]], _NT,
                            preferred_element_type=F32)
        s = jnp.where(causal, s, MASK_VALUE)
        m = jnp.max(s, axis=1, keepdims=True)
        p = jnp.exp2(s - m)
        v_own = jnp.concatenate([v_ref[0, pl.ds(own, tq), cols[h]], ones[:tq]], axis=1)
        state += [m, jnp.dot(p.astype(BF16), v_own, preferred_element_type=F32)]
    lane = lax.broadcasted_iota(jnp.int32, (span * tq, HEAD_DIM), 1)
    block = lax.broadcasted_iota(jnp.int32, (span * tq, HEAD_DIM), 0) // tq

    def body(g, carry):
        off = pl.multiple_of(g * (span * tq), span * tq)
        one_hot = jnp.where(lane == g * span + block, 1.0, 0.0).astype(BF16)
        out = []
        for h in range(n_heads):
            m, acc = carry[2 * h:2 * h + 2]
            k_ext = jnp.concatenate([k_ref[0, pl.ds(off, span * tq), cols[h]], one_hot], axis=1)
            s = lax.dot_general(q_scr[h], k_ext, _NT, preferred_element_type=F32)
            m_new = jnp.maximum(m, jnp.max(s, axis=1, keepdims=True))
            p = jnp.exp2(s - m_new)
            v_ext = jnp.concatenate([v_ref[0, pl.ds(off, span * tq), cols[h]], ones], axis=1)
            acc = jnp.exp2(m - m_new) * acc + jnp.dot(p.astype(BF16), v_ext, preferred_element_type=F32)
            out += [m_new, acc]
        return tuple(out)

    state = lax.fori_loop(0, (i + span - 1) // span, body, tuple(state))
    for h in range(n_heads):
        acc = state[2 * h + 1]
        o_ref[0, :, cols[h]] = (acc[:, :HEAD_DIM] / acc[:, HEAD_DIM:]).astype(o_ref.dtype)


def moba_prompt(x, g, wq, block_sums, k_bf, v_bf):
    n_seq, seq, d = k_bf.shape
    nb = seq // MOBA_BLOCK
    nb8 = -(-nb // 8) * 8
    assert seq % (MOBA_BLOCKS_PER_STEP * MOBA_BLOCK) == 0
    assert nb8 < LANES and d == N_HEADS * HEAD_DIM
    sums = jnp.pad(block_sums.reshape(n_seq, nb, d), ((0, 0), (0, nb8 - nb), (0, 0)))
    kv = pl.BlockSpec((1, seq, d), lambda b, i: (b, 0, 0), pipeline_mode=pl.Buffered(1))
    out = pl.pallas_call(
        _moba_kernel,
        grid=(n_seq, nb),
        in_specs=[pl.BlockSpec((MOBA_BLOCK, d), lambda b, i: (b * nb + i, 0)),
                  _resident((1, d)), _resident((d, d)),
                  pl.BlockSpec((1, nb8, d), lambda b, i: (b, 0, 0)), kv, kv],
        out_specs=pl.BlockSpec((1, MOBA_BLOCK, d), lambda b, i: (b, i, 0)),
        out_shape=jax.ShapeDtypeStruct((n_seq, seq, d), BF16),
        scratch_shapes=[pltpu.VMEM((N_HEADS, MOBA_BLOCK, 2 * HEAD_DIM), BF16)],
        compiler_params=_params("parallel", "arbitrary"),
        name="moba_prompt",
    )(x, g.reshape(1, d), wq, sums, k_bf, v_bf)
    return out.reshape(n_seq * seq, d)


def _sample_select_kernel(q_ref, sums_ref, o_ref):
    for s in range(q_ref.shape[0]):
        prod = q_ref[s] * (sums_ref[s] * (1.0 / MOBA_BLOCK))
        gate = jnp.sum(prod, axis=2, keepdims=True)
        nb = gate.shape[0]
        blk = lax.broadcasted_iota(jnp.int32, gate.shape, 0).astype(F32)
        for j in range(TOP_K):
            m = jnp.max(gate, axis=0, keepdims=True)
            first = jnp.min(jnp.where(gate == m, blk, float(nb - 1)), axis=0, keepdims=True)
            o_ref[s, j] = jnp.broadcast_to(first[0], o_ref.shape[2:]).astype(jnp.int32)
            gate = jnp.where(blk == first, -jnp.inf, gate)


def sample_select(q, sums):
    b, nh, hd = q.shape
    nb = sums.shape[1]
    assert nb >= TOP_K
    grp = SELECT_SEQS_PER_STEP if b % SELECT_SEQS_PER_STEP == 0 else b
    out = pl.pallas_call(
        _sample_select_kernel,
        grid=(b // grp,),
        in_specs=[pl.BlockSpec((grp, nh, hd), lambda s: (s, 0, 0)),
                  pl.BlockSpec((grp, nb, nh, hd), lambda s: (s, 0, 0, 0))],
        out_specs=pl.BlockSpec((grp, TOP_K, nh, hd), lambda s: (s, 0, 0, 0)),
        out_shape=jax.ShapeDtypeStruct((b, TOP_K, nh, hd), jnp.int32),
        compiler_params=_params("parallel"),
        name="sample_select",
    )(q, sums)
    return out[:, :, :, 0].transpose(0, 2, 1)


def _sample_attn_kernel(n_pages, sel_ref, pt_ref, q_ref, kn_ref, vn_ref, ck_ref, cv_ref, o_ref,
                        kbuf, vbuf, sem):
    step = pl.program_id(0)
    per_block = MOBA_BLOCK // PAGE_SIZE

    def copies(seq, slot):
        out = []
        for h in range(N_HEADS):
            for j in range(n_pages):
                page = pt_ref[seq, sel_ref[seq, h, j // per_block] * per_block + j % per_block]
                out.append(pltpu.make_async_copy(ck_ref.at[page, :, h, :], kbuf.at[slot, h, j], sem.at[slot]))
                out.append(pltpu.make_async_copy(cv_ref.at[page, :, h, :], vbuf.at[slot, h, j], sem.at[slot]))
        return out

    slot = step % 2

    @pl.when(step == 0)
    def _():
        for c in copies(0, 0):
            c.start()

    @pl.when(step + 1 < pl.num_programs(0))
    def _():
        for c in copies(step + 1, 1 - slot):
            c.start()

    for c in copies(step, slot):
        c.wait()

    q = q_ref[0] * ATTN_SCALE
    rows = []
    for h in range(N_HEADS):
        qh = q[h:h + 1]
        s_new = jnp.sum(qh * kn_ref[0, h:h + 1], axis=1, keepdims=True)
        scores = [jnp.sum(kbuf[slot, h, j] * qh, axis=1, keepdims=True) for j in range(n_pages)]
        m = s_new
        for s in scores:
            m = jnp.maximum(m, jnp.max(s, axis=0, keepdims=True))
        p_new = jnp.exp(s_new - m)
        l = p_new
        acc = p_new * vn_ref[0, h:h + 1]
        for j, s in enumerate(scores):
            p = jnp.exp(s - m)
            l = l + jnp.sum(p, axis=0, keepdims=True)
            acc = acc + jnp.sum(p * vbuf[slot, h, j], axis=0, keepdims=True)
        rows.append(acc / l)
    o_ref[0] = jnp.concatenate(rows, axis=0)


def moba_sample(q, k_new, v_new, cache_k, cache_v, page_table, sel):
    b, nh, hd = q.shape
    ps = cache_k.shape[1]
    n_pages = TOP_K * (MOBA_BLOCK // ps)
    seq = pl.BlockSpec((1, nh, hd), lambda s, sel_ref, pt_ref: (s, 0, 0))
    hbm = pl.BlockSpec(memory_space=pl.ANY)
    return pl.pallas_call(
        functools.partial(_sample_attn_kernel, n_pages),
        grid_spec=pltpu.PrefetchScalarGridSpec(
            num_scalar_prefetch=2,
            grid=(b,),
            in_specs=[seq, seq, seq, hbm, hbm],
            out_specs=seq,
            scratch_shapes=[pltpu.VMEM((2, nh, n_pages, ps, hd), F32),
                            pltpu.VMEM((2, nh, n_pages, ps, hd), F32),
                            pltpu.SemaphoreType.DMA((2,))],
        ),
        out_shape=jax.ShapeDtypeStruct((b, nh, hd), F32),
        compiler_params=_params("arbitrary"),
        name="moba_sample",
    )(sel, page_table, q, k_new, v_new, cache_k, cache_v)


def kernel(x_prompt, x_sample, state_ssm_re, state_ssm_im, cache_k, cache_v, page_table,
           norm_ffn1, w_ffn1_gate, w_ffn1_up, w_ffn1_down, norm_mix,
           norm_ffn2, w_ffn2_gate, w_ffn2_up, w_ffn2_down,
           ssm_a_re, ssm_a_im, ssm_log_step, ssm_b_re, ssm_b_im, ssm_c_re, ssm_c_im,
           ssm_d, ssm_w_glu_a, ssm_w_glu_b,
           norm_kv, w_k, w_v, attn_w_q, attn_w_o, norm_final):
    n_seq, seq, d = x_prompt.shape
    n_dec, dec_seq, _ = x_sample.shape
    assert dec_seq == 1 and norm_ffn1.shape[0] == 2 and ssm_a_re.shape[0] == 1
    n_pool, page_size, n_heads, head_dim = cache_k.shape
    assert (page_table.shape[1] * page_size) % MOBA_BLOCK == 0

    bf = lambda w: w.astype(BF16)
    ffn1 = (norm_ffn1, bf(w_ffn1_gate), bf(w_ffn1_up), bf(w_ffn1_down))
    ffn2 = (norm_ffn2, bf(w_ffn2_gate), bf(w_ffn2_up), bf(w_ffn2_down))
    glu_a, glu_b = bf(ssm_w_glu_a[0]), bf(ssm_w_glu_b[0])
    wk, wv, wq, wo = bf(w_k), bf(w_v), bf(attn_w_q[0]), bf(attn_w_o[0])
    opw, bst, cst, lb, m0, b0, c1 = ssm_weights(ssm_a_re[0], ssm_a_im[0], ssm_log_step[0], ssm_b_re[0],
                                                ssm_b_im[0], ssm_c_re[0], ssm_c_im[0])

    n_host = 4
    assert n_dec % n_host == 0
    block_sums = []

    def prompt_ffn(x, w, layer, **post):
        first = len(block_sums) * (n_dec // n_host)
        *out, sums = half_ffn_with_page_sums(x, *w, layer, cache_k, page_table, first, n_dec // n_host, **post)
        block_sums.append(sums)
        return out[0] if len(out) == 1 else out

    xp = x_prompt.reshape(n_seq * seq, d)
    xp, u = prompt_ffn(xp, ffn1, 0, post="x+planes", g_post=norm_mix[0])
    y, p_re, p_im = s5_prompt(u, n_seq, opw, bst, cst, lb)
    xp = prompt_ffn(xp, ffn2, 0, pre=("glu", y, norm_mix[0], ssm_d[0], glu_a, glu_b))
    k_p, v_p, kb_p, vb_p, ksum_p = shared_kv(xp, norm_kv, wk, wv, True)
    xp = prompt_ffn(xp, ffn1, 1)
    attn_p = moba_prompt(xp, norm_mix[1], wq, ksum_p, kb_p.reshape(n_seq, seq, d), vb_p.reshape(n_seq, seq, d))
    y_prompt = prompt_ffn(xp, ffn2, 1, pre=("proj", attn_p, wo), post="normed",
                          g_post=norm_final).reshape(n_seq, seq, d)

    xs = x_sample.reshape(n_dec, d)
    xs = half_ffn(xs, *ffn1, 0)
    y, s_re, s_im = s5_step(rmsnorm(xs, norm_mix[0], F32), state_ssm_re[0], state_ssm_im[0], lb, m0, b0, c1)
    xs = ssm_glu(y, xs, norm_mix[0], ssm_d[0], glu_a, glu_b)
    xs = half_ffn(xs, *ffn2, 0)
    k_s, v_s, _, _ = shared_kv(xs, norm_kv, wk, wv, False)
    xs = half_ffn(xs, *ffn1, 1)
    q_s = norm_proj(xs, norm_mix[1], wq)
    per_head = lambda a: a.reshape(n_dec, n_heads, head_dim)
    sel = sample_select(per_head(q_s), jnp.concatenate(block_sums, axis=0))
    attn_s = moba_sample(per_head(q_s), per_head(k_s), per_head(v_s), cache_k, cache_v, page_table, sel)
    xs = proj_residual(attn_s.reshape(n_dec, d).astype(BF16), wo, xs)
    xs = half_ffn(xs, *ffn2, 1)
    y_sample = rmsnorm(xs, norm_final, F32).reshape(n_dec, 1, d)

    heads = lambda a, b, s: a.reshape(b, s, n_heads, head_dim)
    return (y_prompt, y_sample, p_re[None], p_im[None], s_re[None], s_im[None],
            heads(k_p, n_seq, seq), heads(v_p, n_seq, seq), heads(k_s, n_dec, 1), heads(v_s, n_dec, 1))
```

```python
import functools

import jax
import jax.numpy as jnp
from jax import lax
from jax.experimental import pallas as pl
from jax.experimental.pallas import tpu as pltpu

F32 = jnp.float32
BF16 = jnp.bfloat16

D_MODEL = 1024
N_GROUPS = 64
GROUP_SIZE = 16
STATE_DIM = 64
N_HEADS = 8
HEAD_DIM = 128
MOBA_BLOCK = 256
TOP_K = 3
PAGE_SIZE = 128
RMS_EPS = 1e-6
ATTN_SCALE = HEAD_DIM ** -0.5
LOG2_E = 1.4426950408889634
MASK_VALUE = -1e30

LANES = 128
SSM_CHUNK = 8
GROUPS_PER_BLOCK = LANES // GROUP_SIZE
BLOCK_STATE = GROUPS_PER_BLOCK * 2 * STATE_DIM
ROW_TILE = 512
FFN_PAGES_ROW_TILE = 256
FFN_STREAM_CHUNK = 256
SSM_CHUNK_TILE = 1024
SCAN_TILE = 128
MOBA_HEADS_PER_STEP = 8
SELECT_SEQS_PER_STEP = 8
MOBA_BLOCKS_PER_STEP = 4
VMEM_LIMIT = 56 * 1024 * 1024

_NT = (((1,), (1,)), ((), ()))


def _params(*sem):
    return pltpu.CompilerParams(dimension_semantics=sem, vmem_limit_bytes=VMEM_LIMIT)


def _resident(shape):
    zeros = (0,) * len(shape)
    return pl.BlockSpec(shape, lambda *_: zeros, pipeline_mode=pl.Buffered(1))


def _rms(x, g):
    inv = lax.rsqrt(jnp.mean(x * x, axis=-1, keepdims=True) + RMS_EPS)
    return x * inv * g


def _row_tile(m):
    return ROW_TILE if m % ROW_TILE == 0 else m


def _ffn_value(x, g_ref, wg_ref, wu_ref, wd_ref):
    h = _rms(x, g_ref[...]).astype(BF16)
    a = jnp.dot(h, wg_ref[0], preferred_element_type=F32)
    b = jnp.dot(h, wu_ref[0], preferred_element_type=F32)
    act = (jax.nn.silu(a) * b).astype(BF16)
    return x + 0.5 * jnp.dot(act, wd_ref[0], preferred_element_type=F32)


def _ffn_chunk_kernel(x_ref, g_ref, wg_ref, wu_ref, wd_ref, o_ref, h_scr, acc_scr):
    j = pl.program_id(1)

    @pl.when(j == 0)
    def _():
        h_scr[...] = _rms(x_ref[...], g_ref[...]).astype(BF16)
        acc_scr[...] = jnp.zeros_like(acc_scr)

    h = h_scr[...]
    a = jnp.dot(h, wg_ref[0], preferred_element_type=F32)
    b = jnp.dot(h, wu_ref[0], preferred_element_type=F32)
    act = (jax.nn.silu(a) * b).astype(BF16)
    acc_scr[...] += jnp.dot(act, wd_ref[0], preferred_element_type=F32)

    @pl.when(j == pl.num_programs(1) - 1)
    def _():
        o_ref[...] = x_ref[...] + 0.5 * acc_scr[...]


def _layer_weights(layer, w):
    return pl.BlockSpec((1,) + w.shape[1:], lambda *_: (layer, 0, 0), pipeline_mode=pl.Buffered(1))


def half_ffn(x, g, wg, wu, wd, layer):
    m, d = x.shape
    f = wg.shape[2]
    tm = _row_tile(m)
    fc = FFN_STREAM_CHUNK if f % FFN_STREAM_CHUNK == 0 else f
    row = pl.BlockSpec((tm, d), lambda i, j: (i, 0))
    return pl.pallas_call(
        _ffn_chunk_kernel,
        grid=(m // tm, f // fc),
        in_specs=[row, pl.BlockSpec((1, d), lambda i, j: (0, 0)),
                  pl.BlockSpec((1, d, fc), lambda i, j: (layer, 0, j)),
                  pl.BlockSpec((1, d, fc), lambda i, j: (layer, 0, j)),
                  pl.BlockSpec((1, fc, d), lambda i, j: (layer, j, 0))],
        out_specs=row,
        out_shape=jax.ShapeDtypeStruct((m, d), F32),
        scratch_shapes=[pltpu.VMEM((tm, d), BF16), pltpu.VMEM((tm, d), F32)],
        compiler_params=_params("parallel", "arbitrary"),
        name="half_ffn",
    )(x, g[layer].reshape(1, d), wg, wu, wd)


_PRE_OPERANDS = {"none": 0, "proj": 2, "glu": 5}


def _ffn_pages_kernel(n_pages, pre, post, pt_ref, x_ref, g_ref, wg_ref, wu_ref, wd_ref, g2_ref, *refs):
    n_pre = _PRE_OPERANDS[pre]
    pre_refs, pages, outs = refs[:n_pre], refs[n_pre:n_pre + n_pages], refs[n_pre + n_pages:]
    sums_ref = outs[-1]
    per_block = MOBA_BLOCK // PAGE_SIZE
    for r in range(n_pages // per_block):
        total = jnp.sum(pages[r * per_block][0], axis=0)
        for j in range(1, per_block):
            total = total + jnp.sum(pages[r * per_block + j][0], axis=0)
        sums_ref[0, r] = total
    x = x_ref[...]
    if pre == "proj":
        a_ref, w_ref = pre_refs
        x = x + jnp.dot(a_ref[...], w_ref[...], preferred_element_type=F32)
    elif pre == "glu":
        x = _glu_value(pre_refs[0], x, *pre_refs[1:])
    x_new = _ffn_value(x, g_ref, wg_ref, wu_ref, wd_ref)
    if post == "normed":
        outs[0][...] = _rms(x_new, g2_ref[...])
        return
    outs[0][...] = x_new
    if post == "x+planes":
        u = _rms(x_new, g2_ref[...])
        for v in range(outs[1].shape[0]):
            outs[1][v] = u[:, v * LANES:(v + 1) * LANES]


def half_ffn_with_page_sums(x, g, wg, wu, wd, layer, cache_k, page_table, first_seq, n_seqs,
                            pre=("none",), post="x", g_post=None):
    m, d = x.shape
    tm = FFN_PAGES_ROW_TILE if m % FFN_PAGES_ROW_TILE == 0 else m
    steps = m // tm
    n_pool, ps, nh, hd = cache_k.shape
    n_pages = page_table.shape[1]
    per_block = MOBA_BLOCK // ps
    assert ps == PAGE_SIZE and (n_seqs * n_pages) % steps == 0
    per_step = n_seqs * n_pages // steps
    assert per_step % per_block == 0 and n_pages % per_step == 0
    steps_per_seq = n_pages // per_step
    row = pl.BlockSpec((tm, d), lambda i, pt: (i, 0))
    vec = pl.BlockSpec((1, d), lambda i, pt: (0, 0), pipeline_mode=pl.Buffered(1))
    mat = pl.BlockSpec((d, d), lambda i, pt: (0, 0), pipeline_mode=pl.Buffered(1))
    planes = pl.BlockSpec((d // LANES, tm, LANES), lambda i, pt: (0, i, 0))
    lw = lambda w: _layer_weights(layer, w)
    page = lambda r: pl.BlockSpec(
        (1, ps, nh, hd),
        lambda i, pt: (pt[first_seq + i // steps_per_seq, (i % steps_per_seq) * per_step + r], 0, 0, 0))
    kind, *pre_args = pre
    assert len(pre_args) == _PRE_OPERANDS[kind]
    if kind == "proj":
        pre_specs = [row, mat]
    elif kind == "glu":
        pre_specs = [planes, vec, vec, mat, mat]
        pre_args = [pre_args[0], pre_args[1].reshape(1, d), pre_args[2].reshape(1, d), *pre_args[3:]]
    else:
        pre_specs = []
    out_specs, out_shape = [row], [jax.ShapeDtypeStruct((m, d), F32)]
    if post == "x+planes":
        out_specs.append(planes)
        out_shape.append(jax.ShapeDtypeStruct((d // LANES, m, LANES), F32))
    out_specs.append(pl.BlockSpec((1, per_step // per_block, nh, hd),
                                  lambda i, pt: (i // steps_per_seq, i % steps_per_seq, 0, 0)))
    out_shape.append(jax.ShapeDtypeStruct((n_seqs, n_pages // per_block, nh, hd), F32))
    g_post = g[layer] if g_post is None else g_post
    return pl.pallas_call(
        functools.partial(_ffn_pages_kernel, per_step, kind, post),
        grid_spec=pltpu.PrefetchScalarGridSpec(
            num_scalar_prefetch=1,
            grid=(steps,),
            in_specs=[row, vec, lw(wg), lw(wu), lw(wd), vec] + pre_specs + [page(r) for r in range(per_step)],
            out_specs=out_specs,
        ),
        out_shape=out_shape,
        compiler_params=_params("parallel"),
        name="half_ffn_page_sums",
    )(page_table, x, g[layer].reshape(1, d), wg, wu, wd, g_post.reshape(1, d), *pre_args,
      *([cache_k] * per_step))


def _rmsnorm_kernel(x_ref, g_ref, o_ref):
    o_ref[...] = _rms(x_ref[...], g_ref[...]).astype(o_ref.dtype)


def rmsnorm(x, g, dtype):
    m, d = x.shape
    tm = _row_tile(m)
    row = pl.BlockSpec((tm, d), lambda i: (i, 0))
    return pl.pallas_call(
        _rmsnorm_kernel,
        grid=(m // tm,),
        in_specs=[row, _resident((1, d))],
        out_specs=row,
        out_shape=jax.ShapeDtypeStruct((m, d), dtype),
        compiler_params=_params("parallel"),
        name="rmsnorm",
    )(x, g.reshape(1, d))


def _glu_value(y_ref, x, g_ref, d_ref, wa_ref, wb_ref):
    u = _rms(x, g_ref[...])
    y = jnp.concatenate([y_ref[v] for v in range(y_ref.shape[0])], axis=1)
    act = jax.nn.gelu(y + d_ref[...] * u).astype(BF16)
    a = jnp.dot(act, wa_ref[...], preferred_element_type=F32)
    b = jnp.dot(act, wb_ref[...], preferred_element_type=F32)
    return x + a * jax.nn.sigmoid(b)


def _glu_kernel(y_ref, x_ref, g_ref, d_ref, wa_ref, wb_ref, o_ref):
    o_ref[...] = _glu_value(y_ref, x_ref[...], g_ref, d_ref, wa_ref, wb_ref)


def ssm_glu(y, x, g_mix, d_skip, wa, wb):
    m, d = x.shape
    tm = _row_tile(m)
    row = pl.BlockSpec((tm, d), lambda i: (i, 0))
    planes = pl.BlockSpec((d // LANES, tm, LANES), lambda i: (0, i, 0))
    return pl.pallas_call(
        _glu_kernel,
        grid=(m // tm,),
        in_specs=[planes, row, _resident((1, d)), _resident((1, d)), _resident((d, d)), _resident((d, d))],
        out_specs=row,
        out_shape=jax.ShapeDtypeStruct((m, d), F32),
        compiler_params=_params("parallel"),
        name="ssm_glu",
    )(y, x, g_mix.reshape(1, d), d_skip.reshape(1, d), wa, wb)


def _kv_kernel(n_sum, x_ref, g_ref, wk_ref, wv_ref, k_ref, v_ref, kb_ref, vb_ref, *sum_ref):
    h = _rms(x_ref[...], g_ref[...]).astype(BF16)
    k = jnp.dot(h, wk_ref[...], preferred_element_type=F32)
    v = jnp.dot(h, wv_ref[...], preferred_element_type=F32)
    k_ref[...] = k
    v_ref[...] = v
    kb_ref[...] = k.astype(BF16)
    vb_ref[...] = v.astype(BF16)
    for r in range(n_sum):
        sum_ref[0][r] = jnp.sum(k[r * MOBA_BLOCK:(r + 1) * MOBA_BLOCK], axis=0, keepdims=True)


def shared_kv(x, g, wk, wv, with_block_sums):
    m, d = x.shape
    tm = _row_tile(m)
    n_sum = tm // MOBA_BLOCK if with_block_sums else 0
    row = pl.BlockSpec((tm, d), lambda i: (i, 0))
    out_specs = [row, row, row, row]
    out_shape = [jax.ShapeDtypeStruct((m, d), F32)] * 2 + [jax.ShapeDtypeStruct((m, d), BF16)] * 2
    if with_block_sums:
        assert tm % MOBA_BLOCK == 0
        out_specs.append(pl.BlockSpec((n_sum, 1, d), lambda i: (i, 0, 0)))
        out_shape.append(jax.ShapeDtypeStruct((m // MOBA_BLOCK, 1, d), F32))
    return pl.pallas_call(
        functools.partial(_kv_kernel, n_sum),
        grid=(m // tm,),
        in_specs=[row, _resident((1, d)), _resident((d, d)), _resident((d, d))],
        out_specs=out_specs,
        out_shape=out_shape,
        compiler_params=_params("parallel"),
        name="shared_kv",
    )(x, g.reshape(1, d), wk, wv)


def _proj_residual_kernel(a_ref, w_ref, x_ref, o_ref):
    o_ref[...] = x_ref[...] + jnp.dot(a_ref[...], w_ref[...], preferred_element_type=F32)


def proj_residual(a, w, x):
    m, d = x.shape
    tm = _row_tile(m)
    row = pl.BlockSpec((tm, d), lambda i: (i, 0))
    return pl.pallas_call(
        _proj_residual_kernel,
        grid=(m // tm,),
        in_specs=[row, _resident((d, d)), row],
        out_specs=row,
        out_shape=jax.ShapeDtypeStruct((m, d), F32),
        compiler_params=_params("parallel"),
        name="proj_residual",
    )(a, w, x)


def _norm_proj_kernel(x_ref, g_ref, w_ref, o_ref):
    h = _rms(x_ref[...], g_ref[...]).astype(BF16)
    o_ref[...] = jnp.dot(h, w_ref[...], preferred_element_type=F32)


def norm_proj(x, g, w):
    m, d = x.shape
    tm = _row_tile(m)
    row = pl.BlockSpec((tm, d), lambda i: (i, 0))
    return pl.pallas_call(
        _norm_proj_kernel,
        grid=(m // tm,),
        in_specs=[row, _resident((1, d)), _resident((d, d))],
        out_specs=row,
        out_shape=jax.ShapeDtypeStruct((m, d), F32),
        compiler_params=_params("parallel"),
        name="norm_proj",
    )(x, g.reshape(1, d), w)


def _ssm_weights_kernel(are_ref, aim_ref, ls_ref, bre_ref, bim_ref, cre_ref, cim_ref,
                        opw_ref, bst_ref, cst_ref, lb_ref, m0_ref, b0_ref, c1_ref):
    p, gs, t_len = STATE_DIM, GROUP_SIZE, SSM_CHUNK
    zeros = lambda r, c: jnp.zeros((r, c), F32)
    for a in range(GROUPS_PER_BLOCK):
        lam_re = jnp.minimum(are_ref[a], -1e-4)
        lam_im = aim_ref[a]
        dt = jnp.exp(ls_ref[a])
        lag = lax.broadcasted_iota(jnp.int32, (t_len + 8, p), 0).astype(F32)
        mag = jnp.exp(lag * (lam_re * dt))
        ang = lag * (lam_im * dt)
        pw_re = mag * jnp.cos(ang)
        pw_im = mag * jnp.sin(ang)
        lb_re = pw_re[1:2]
        lb_im = pw_im[1:2]
        den = lam_re * lam_re + lam_im * lam_im
        nr = lb_re - 1.0
        coef_re = (nr * lam_re + lb_im * lam_im) / den
        coef_im = (lb_im * lam_re - nr * lam_im) / den
        bt_re = bre_ref[a]
        bt_im = bim_ref[a]
        bb_re = coef_re * bt_re - coef_im * bt_im
        bb_im = coef_re * bt_im + coef_im * bt_re
        bb_cat = jnp.concatenate([bb_re, bb_im], axis=1)
        c_re = cre_ref[a]
        c_im = cim_ref[a]

        def c_pow(t):
            re = c_re * pw_re[t:t + 1] - c_im * pw_im[t:t + 1]
            im = c_re * pw_im[t:t + 1] + c_im * pw_re[t:t + 1]
            return jnp.concatenate([re, -im], axis=1)

        def in_block_rows(x):
            parts = ([zeros(a * gs, x.shape[1])] if a else []) + [x]
            rest = LANES - (a + 1) * gs
            return jnp.concatenate(parts + ([zeros(rest, x.shape[1])] if rest else []), axis=0)

        def in_block_cols(x):
            parts = ([zeros(gs, a * 2 * p)] if a else []) + [x]
            rest = BLOCK_STATE - (a + 1) * 2 * p
            return jnp.concatenate(parts + ([zeros(gs, rest)] if rest else []), axis=1)

        cp = [c_pow(t) for t in range(t_len + 1)]
        spread = jnp.concatenate([in_block_rows(cp[t]) for t in range(t_len)], axis=0)
        mt = lax.dot_general(bb_cat, spread, _NT, precision=lax.Precision.HIGHEST,
                             preferred_element_type=F32)
        for s in range(t_len):
            rows = slice(s * LANES + a * gs, s * LANES + (a + 1) * gs)
            shifted = mt if s == 0 else jnp.concatenate(
                [zeros(gs, s * LANES), mt[:, :(t_len - s) * LANES]], axis=1)
            opw_ref[0, rows, :] = shifted.astype(opw_ref.dtype)
            q = t_len - 1 - s
            re = bb_re * pw_re[q:q + 1] - bb_im * pw_im[q:q + 1]
            im = bb_re * pw_im[q:q + 1] + bb_im * pw_re[q:q + 1]
            bst_ref[0, rows, :] = in_block_cols(jnp.concatenate([re, im], axis=1)).astype(bst_ref.dtype)
            cst_ref[0, rows, :] = in_block_cols(cp[s + 1]).astype(cst_ref.dtype)
        lb_ref[a, 0:1, :] = jnp.concatenate([pw_re[t_len:t_len + 1], pw_im[t_len:t_len + 1]], axis=1)
        lb_ref[a, 1:2, :] = jnp.concatenate([lb_re, lb_im], axis=1)
        lb_ref[a, 2:8, :] = zeros(6, 2 * p)
        m0_ref[a] = mt[:, :LANES]
        b0_ref[a] = bb_cat
        c1_ref[a] = cp[1]


def ssm_weights(a_re, a_im, log_step, b_re, b_im, c_re, c_im):
    g, p, gs, gb = N_GROUPS, STATE_DIM, GROUP_SIZE, GROUPS_PER_BLOCK
    tl = SSM_CHUNK * LANES
    grp = lambda r, c: pl.BlockSpec((gb, r, c), lambda i: (i, 0, 0))
    blk = lambda r, c: pl.BlockSpec((1, r, c), lambda i: (i, 0, 0))
    return pl.pallas_call(
        _ssm_weights_kernel,
        grid=(g // gb,),
        in_specs=[grp(1, p), grp(1, p), grp(1, 1), grp(gs, p), grp(gs, p), grp(gs, p), grp(gs, p)],
        out_specs=[blk(tl, tl), blk(tl, BLOCK_STATE), blk(tl, BLOCK_STATE),
                   grp(8, 2 * p), grp(gs, LANES), grp(gs, 2 * p), grp(gs, 2 * p)],
        out_shape=[jax.ShapeDtypeStruct((g // gb, tl, tl), BF16),
                   jax.ShapeDtypeStruct((g // gb, tl, BLOCK_STATE), BF16),
                   jax.ShapeDtypeStruct((g // gb, tl, BLOCK_STATE), BF16),
                   jax.ShapeDtypeStruct((g, 8, 2 * p), F32),
                   jax.ShapeDtypeStruct((g, gs, LANES), F32),
                   jax.ShapeDtypeStruct((g, gs, 2 * p), F32),
                   jax.ShapeDtypeStruct((g, gs, 2 * p), F32)],
        compiler_params=_params("parallel"),
        name="ssm_weights",
    )(a_re.reshape(g, 1, p), a_im.reshape(g, 1, p), log_step.reshape(g, 1, 1),
      b_re.transpose(0, 2, 1), b_im.transpose(0, 2, 1), c_re, c_im)


def _lane_block(u_ref, n_chunks):
    return jnp.concatenate(
        [u_ref[0, pl.ds(t, n_chunks, stride=SSM_CHUNK), :] for t in range(SSM_CHUNK)], axis=1).astype(BF16)


def _chunk_state_kernel(n_chunks, u_ref, bst_ref, e_ref):
    e_ref[0] = jnp.dot(_lane_block(u_ref, n_chunks), bst_ref[0], preferred_element_type=F32)


def _chunk_tile(nc):
    return SSM_CHUNK_TILE if nc % SSM_CHUNK_TILE == 0 else nc


def chunk_states(u, bst):
    nv, m, _ = u.shape
    nc = m // SSM_CHUNK
    nct = _chunk_tile(nc)
    return pl.pallas_call(
        functools.partial(_chunk_state_kernel, nct),
        grid=(nv, nc // nct),
        in_specs=[pl.BlockSpec((1, nct * SSM_CHUNK, LANES), lambda v, i: (v, i, 0)),
                  pl.BlockSpec((1,) + bst.shape[1:], lambda v, i: (v, 0, 0))],
        out_specs=pl.BlockSpec((1, nct, BLOCK_STATE), lambda v, i: (v, i, 0)),
        out_shape=jax.ShapeDtypeStruct((nv, nc, BLOCK_STATE), F32),
        compiler_params=_params("parallel", "parallel"),
        name="ssm_chunk_states",
    )(u, bst)


def _swap_halves(x):
    half = STATE_DIM
    n = x.shape[-1]
    lane = lax.broadcasted_iota(jnp.int32, x.shape, x.ndim - 1)
    return jnp.where(lane % (2 * half) < half, pltpu.roll(x, n - half, axis=x.ndim - 1),
                     pltpu.roll(x, half, axis=x.ndim - 1))


def _scan_kernel(n_steps, e_ref, h0_ref, a_ref, hin_ref, hfin_ref, h_scr, hs_scr, e_scr, hin_scr):
    @pl.when(pl.program_id(1) == 0)
    def _():
        h_scr[...] = h0_ref[0]
        hs_scr[...] = _swap_halves(h0_ref[0])

    e_scr[...] = jnp.swapaxes(e_ref[...], 0, 1)
    a_same = a_ref[0]
    a_cross = a_ref[1]
    a_cross_s = a_ref[2]

    def body(t, carry):
        h, hs = carry
        e = e_scr[t]
        es = _swap_halves(e)
        hin_scr[t] = h
        return a_same * h + a_cross * hs + e, a_same * hs + a_cross_s * h + es

    h, hs = lax.fori_loop(0, n_steps, body, (h_scr[...], hs_scr[...]), unroll=8)
    h_scr[...] = h
    hs_scr[...] = hs
    hfin_ref[0] = h
    hin_ref[...] = jnp.swapaxes(hin_scr[...], 0, 1).astype(hin_ref.dtype)


def chunk_scan(e, h0, a):
    nv, nc, w = e.shape
    n_seq = h0.shape[0]
    per_seq = nc // n_seq
    ct = SCAN_TILE if per_seq % SCAN_TILE == 0 else per_seq
    n_ct = per_seq // ct
    tile = pl.BlockSpec((nv, ct, w), lambda b, c: (0, b * n_ct + c, 0))
    seq = pl.BlockSpec((1, nv, w), lambda b, c: (b, 0, 0))
    return pl.pallas_call(
        functools.partial(_scan_kernel, ct),
        grid=(n_seq, n_ct),
        in_specs=[tile, seq, _resident((3, nv, w))],
        out_specs=[tile, seq],
        out_shape=[jax.ShapeDtypeStruct((nv, nc, w), BF16), jax.ShapeDtypeStruct((n_seq, nv, w), F32)],
        scratch_shapes=[pltpu.VMEM((nv, w), F32), pltpu.VMEM((nv, w), F32),
                        pltpu.VMEM((ct, nv, w), F32), pltpu.VMEM((ct, nv, w), F32)],
        compiler_params=_params("parallel", "arbitrary"),
        name="ssm_chunk_scan",
    )(e, h0, a)


def _chunk_out_kernel(n_chunks, u_ref, hin_ref, opw_ref, cst_ref, y_ref):
    w = _lane_block(u_ref, n_chunks)
    hin = hin_ref[0]
    tile = 2 * LANES
    for c in range(SSM_CHUNK * LANES // tile):
        cols = slice(c * tile, (c + 1) * tile)
        k = (c + 1) * tile
        y = jnp.dot(w[:, :k], opw_ref[0, :k, cols], preferred_element_type=F32)
        y = y + lax.dot_general(hin, cst_ref[0, cols, :], _NT, preferred_element_type=F32)
        for j in range(tile // LANES):
            t = c * (tile // LANES) + j
            y_ref[0, pl.ds(t, n_chunks, stride=SSM_CHUNK), :] = y[:, j * LANES:(j + 1) * LANES]


def chunk_outputs(u, hin, opw, cst):
    m = u.shape[1]
    nv, nc, w = hin.shape
    nct = _chunk_tile(nc)
    rows = nct * SSM_CHUNK
    per_block = lambda a: pl.BlockSpec((1,) + a.shape[1:], lambda v, i: (v, 0, 0))
    return pl.pallas_call(
        functools.partial(_chunk_out_kernel, nct),
        grid=(nv, nc // nct),
        in_specs=[pl.BlockSpec((1, rows, LANES), lambda v, i: (v, i, 0)),
                  pl.BlockSpec((1, nct, w), lambda v, i: (v, i, 0)), per_block(opw), per_block(cst)],
        out_specs=pl.BlockSpec((1, rows, LANES), lambda v, i: (v, i, 0)),
        out_shape=jax.ShapeDtypeStruct((nv, m, LANES), F32),
        compiler_params=_params("parallel", "parallel"),
        name="ssm_chunk_outputs",
    )(u, hin, opw, cst)


def s5_prompt(u, n_seq, opw, bst, cst, lb):
    g, p = N_GROUPS, STATE_DIM
    nv = g // GROUPS_PER_BLOCK
    e = chunk_states(u, bst)
    lt_re, lt_im = lb[:, 0, :p], lb[:, 0, p:]
    per_block = lambda *halves: jnp.concatenate(halves, axis=1).reshape(nv, BLOCK_STATE)
    a = jnp.stack([per_block(lt_re, lt_re), per_block(-lt_im, lt_im), per_block(lt_im, -lt_im)])
    hin, hfin = chunk_scan(e, jnp.zeros((n_seq, nv, BLOCK_STATE), F32), a)
    y = chunk_outputs(u, hin, opw, cst)
    hfin = hfin.reshape(n_seq, g, 2 * p)
    return y, hfin[:, :, :p], hfin[:, :, p:]


def _ssm_step_kernel(u_ref, hre_ref, him_ref, m0_ref, bre_ref, bim_ref, cre_ref, cim_ref,
                     lre_ref, lim_ref, y_ref, ore_ref, oim_ref):
    u = u_ref[...]
    h_re, h_im = hre_ref[...], him_ref[...]
    l_re, l_im = lre_ref[...], lim_ref[...]
    bmm = lambda a, b: jnp.einsum("gbk,gkn->gbn", a, b, preferred_element_type=F32)
    bmm_nt = lambda a, b: jnp.einsum("gbk,gnk->gbn", a, b, preferred_element_type=F32)
    ore_ref[...] = l_re * h_re - l_im * h_im + bmm(u, bre_ref[...])
    oim_ref[...] = l_re * h_im + l_im * h_re + bmm(u, bim_ref[...])
    y_ref[...] = bmm(u, m0_ref[...]) + bmm_nt(h_re, cre_ref[...]) + bmm_nt(h_im, cim_ref[...])


def s5_step(u, h_re, h_im, lb, m0, b0, c1):
    b, d = u.shape
    g, gs, p = N_GROUPS, GROUP_SIZE, STATE_DIM
    m0t = m0.reshape(g, gs, GROUPS_PER_BLOCK, gs).sum(axis=2)
    y, o_re, o_im = pl.pallas_call(
        _ssm_step_kernel,
        out_shape=[jax.ShapeDtypeStruct((g, b, gs), F32),
                   jax.ShapeDtypeStruct((g, b, p), F32), jax.ShapeDtypeStruct((g, b, p), F32)],
        compiler_params=pltpu.CompilerParams(vmem_limit_bytes=VMEM_LIMIT),
        name="ssm_step",
    )(u.reshape(b, g, gs).transpose(1, 0, 2), h_re.transpose(1, 0, 2), h_im.transpose(1, 0, 2),
      m0t, b0[:, :, :p], b0[:, :, p:], c1[:, :, :p], c1[:, :, p:], lb[:, 1:2, :p], lb[:, 1:2, p:])
    y = y.reshape(d // LANES, GROUPS_PER_BLOCK, b, gs).transpose(0, 2, 1, 3).reshape(d // LANES, b, LANES)
    return y, o_re.transpose(1, 0, 2), o_im.transpose(1, 0, 2)


def _select_top_blocks(gate, blk):
    bias = jnp.full(gate.shape, MASK_VALUE, F32)
    for _ in range(TOP_K):
        m = jnp.max(gate, axis=0, keepdims=True)
        first = jnp.min(jnp.where(gate == m, blk, float(gate.shape[0])), axis=0, keepdims=True)
        hit = blk == first
        bias = jnp.where(hit, jnp.where(m > -jnp.inf, 0.0, bias), bias)
        gate = jnp.where(hit, -jnp.inf, gate)
    return bias


def _query_select_kernel(x_ref, g_ref, wq_ref, ks_ref, o_ref):
    own = pl.program_id(1)
    h = _rms(x_ref[...], g_ref[...]).astype(BF16)
    q = jnp.dot(h, wq_ref[...], preferred_element_type=F32)
    qb = q.astype(BF16)
    means = (ks_ref[0] * (1.0 / MOBA_BLOCK)).astype(BF16)
    nb8, tq = means.shape[0], q.shape[0]
    blk = lax.broadcasted_iota(jnp.int32, (nb8, tq), 0)
    past = blk < own
    blk = blk.astype(F32)
    unused = jnp.full((LANES - nb8, tq), MASK_VALUE, F32)
    for hh in range(N_HEADS):
        cols = slice(hh * HEAD_DIM, (hh + 1) * HEAD_DIM)
        gate = lax.dot_general(means[:, cols], qb[:, cols], _NT, preferred_element_type=F32)
        bias = _select_top_blocks(jnp.where(past, gate, -jnp.inf), blk)
        o_ref[0, hh, :, :HEAD_DIM] = (q[:, cols] * (ATTN_SCALE * LOG2_E)).astype(BF16)
        o_ref[0, hh, :, HEAD_DIM:] = jnp.concatenate([bias, unused], axis=0).T.astype(BF16)


def query_select(x, g, wq, block_sums, n_seq):
    m, d = x.shape
    seq = m // n_seq
    nb = seq // MOBA_BLOCK
    nb8 = -(-nb // 8) * 8
    assert seq % MOBA_BLOCK == 0 and nb8 <= LANES
    sums = jnp.pad(block_sums.reshape(n_seq, nb, d), ((0, 0), (0, nb8 - nb), (0, 0)))
    return pl.pallas_call(
        _query_select_kernel,
        grid=(n_seq, nb),
        in_specs=[pl.BlockSpec((MOBA_BLOCK, d), lambda b, i: (b * nb + i, 0)),
                  _resident((1, d)), _resident((d, d)),
                  pl.BlockSpec((1, nb8, d), lambda b, i: (b, 0, 0))],
        out_specs=pl.BlockSpec((1, N_HEADS, MOBA_BLOCK, 2 * HEAD_DIM), lambda b, i: (b, 0, i, 0)),
        out_shape=jax.ShapeDtypeStruct((n_seq, N_HEADS, seq, 2 * HEAD_DIM), BF16),
        compiler_params=_params("parallel", "parallel"),
        name="moba_query_select",
    )(x, g.reshape(1, d), wq, sums)


def _moba_kernel(q_ref, k_ref, v_ref, o_ref):
    i = pl.program_id(2)
    tq = MOBA_BLOCK
    span = MOBA_BLOCKS_PER_STEP
    n_heads = q_ref.shape[1]
    cols = [slice(h * HEAD_DIM, (h + 1) * HEAD_DIM) for h in range(n_heads)]
    own = pl.multiple_of(i * tq, tq)
    row = lax.broadcasted_iota(jnp.int32, (tq, tq), 0)
    col = lax.broadcasted_iota(jnp.int32, (tq, tq), 1)
    causal = col <= row
    ones = jnp.ones((span * tq, HEAD_DIM), BF16)
    state = []
    for h in range(n_heads):
        s = lax.dot_general(q_ref[0, h, :, :HEAD_DIM], k_ref[0, pl.ds(own, tq), cols[h]], _NT,
                            preferred_element_type=F32)
        s = jnp.where(causal, s, MASK_VALUE)
        m = jnp.max(s, axis=1, keepdims=True)
        p = jnp.exp2(s - m)
        v_own = jnp.concatenate([v_ref[0, pl.ds(own, tq), cols[h]], ones[:tq]], axis=1)
        state += [m, jnp.dot(p.astype(BF16), v_own, preferred_element_type=F32)]
    lane = lax.broadcasted_iota(jnp.int32, (span * tq, HEAD_DIM), 1)
    block = lax.broadcasted_iota(jnp.int32, (span * tq, HEAD_DIM), 0) // tq

    def body(g, carry):
        off = pl.multiple_of(g * (span * tq), span * tq)
        one_hot = jnp.where(lane == g * span + block, 1.0, 0.0).astype(BF16)
        out = []
        for h in range(n_heads):
            m, acc = carry[2 * h:2 * h + 2]
            k_ext = jnp.concatenate([k_ref[0, pl.ds(off, span * tq), cols[h]], one_hot], axis=1)
            s = lax.dot_general(q_ref[0, h], k_ext, _NT, preferred_element_type=F32)
            m_new = jnp.maximum(m, jnp.max(s, axis=1, keepdims=True))
            p = jnp.exp2(s - m_new)
            v_ext = jnp.concatenate([v_ref[0, pl.ds(off, span * tq), cols[h]], ones], axis=1)
            acc = jnp.exp2(m - m_new) * acc + jnp.dot(p.astype(BF16), v_ext, preferred_element_type=F32)
            out += [m_new, acc]
        return tuple(out)

    state = lax.fori_loop(0, (i + span - 1) // span, body, tuple(state))
    for h in range(n_heads):
        acc = state[2 * h + 1]
        o_ref[0, :, cols[h]] = (acc[:, :HEAD_DIM] / acc[:, HEAD_DIM:]).astype(o_ref.dtype)


def moba_prompt(q_ext, k_bf, v_bf):
    n_seq, _, seq, _ = q_ext.shape
    d = k_bf.shape[-1]
    nb = seq // MOBA_BLOCK
    hb = MOBA_HEADS_PER_STEP
    assert seq % (MOBA_BLOCKS_PER_STEP * MOBA_BLOCK) == 0
    kv = pl.BlockSpec((1, seq, hb * HEAD_DIM), lambda b, h, i: (b, 0, h), pipeline_mode=pl.Buffered(1))
    out = pl.pallas_call(
        _moba_kernel,
        grid=(n_seq, N_HEADS // hb, nb),
        in_specs=[pl.BlockSpec((1, hb, MOBA_BLOCK, 2 * HEAD_DIM), lambda b, h, i: (b, h, i, 0)), kv, kv],
        out_specs=pl.BlockSpec((1, MOBA_BLOCK, hb * HEAD_DIM), lambda b, h, i: (b, i, h)),
        out_shape=jax.ShapeDtypeStruct((n_seq, seq, d), BF16),
        compiler_params=_params("parallel", "parallel", "arbitrary"),
        name="moba_prompt",
    )(q_ext, k_bf, v_bf)
    return out.reshape(n_seq * seq, d)


def _sample_select_kernel(q_ref, sums_ref, o_ref):
    for s in range(q_ref.shape[0]):
        prod = q_ref[s] * (sums_ref[s] * (1.0 / MOBA_BLOCK))
        gate = jnp.sum(prod, axis=2, keepdims=True)
        nb = gate.shape[0]
        blk = lax.broadcasted_iota(jnp.int32, gate.shape, 0).astype(F32)
        for j in range(TOP_K):
            m = jnp.max(gate, axis=0, keepdims=True)
            first = jnp.min(jnp.where(gate == m, blk, float(nb - 1)), axis=0, keepdims=True)
            o_ref[s, j] = jnp.broadcast_to(first[0], o_ref.shape[2:]).astype(jnp.int32)
            gate = jnp.where(blk == first, -jnp.inf, gate)


def sample_select(q, sums):
    b, nh, hd = q.shape
    nb = sums.shape[1]
    assert nb >= TOP_K
    grp = SELECT_SEQS_PER_STEP if b % SELECT_SEQS_PER_STEP == 0 else b
    out = pl.pallas_call(
        _sample_select_kernel,
        grid=(b // grp,),
        in_specs=[pl.BlockSpec((grp, nh, hd), lambda s: (s, 0, 0)),
                  pl.BlockSpec((grp, nb, nh, hd), lambda s: (s, 0, 0, 0))],
        out_specs=pl.BlockSpec((grp, TOP_K, nh, hd), lambda s: (s, 0, 0, 0)),
        out_shape=jax.ShapeDtypeStruct((b, TOP_K, nh, hd), jnp.int32),
        compiler_params=_params("parallel"),
        name="sample_select",
    )(q, sums)
    return out[:, :, :, 0].transpose(0, 2, 1)


def _sample_attn_kernel(n_pages, sel_ref, pt_ref, q_ref, kn_ref, vn_ref, ck_ref, cv_ref, o_ref,
                        kbuf, vbuf, sem):
    step = pl.program_id(0)
    per_block = MOBA_BLOCK // PAGE_SIZE

    def copies(seq, slot):
        out = []
        for h in range(N_HEADS):
            for j in range(n_pages):
                page = pt_ref[seq, sel_ref[seq, h, j // per_block] * per_block + j % per_block]
                out.append(pltpu.make_async_copy(ck_ref.at[page, :, h, :], kbuf.at[slot, h, j], sem.at[slot]))
                out.append(pltpu.make_async_copy(cv_ref.at[page, :, h, :], vbuf.at[slot, h, j], sem.at[slot]))
        return out

    slot = step % 2

    @pl.when(step == 0)
    def _():
        for c in copies(0, 0):
            c.start()

    @pl.when(step + 1 < pl.num_programs(0))
    def _():
        for c in copies(step + 1, 1 - slot):
            c.start()

    for c in copies(step, slot):
        c.wait()

    q = q_ref[0] * ATTN_SCALE
    rows = []
    for h in range(N_HEADS):
        qh = q[h:h + 1]
        s_new = jnp.sum(qh * kn_ref[0, h:h + 1], axis=1, keepdims=True)
        scores = [jnp.sum(kbuf[slot, h, j] * qh, axis=1, keepdims=True) for j in range(n_pages)]
        m = s_new
        for s in scores:
            m = jnp.maximum(m, jnp.max(s, axis=0, keepdims=True))
        p_new = jnp.exp(s_new - m)
        l = p_new
        acc = p_new * vn_ref[0, h:h + 1]
        for j, s in enumerate(scores):
            p = jnp.exp(s - m)
            l = l + jnp.sum(p, axis=0, keepdims=True)
            acc = acc + jnp.sum(p * vbuf[slot, h, j], axis=0, keepdims=True)
        rows.append(acc / l)
    o_ref[0] = jnp.concatenate(rows, axis=0)


def moba_sample(q, k_new, v_new, cache_k, cache_v, page_table, sel):
    b, nh, hd = q.shape
    ps = cache_k.shape[1]
    n_pages = TOP_K * (MOBA_BLOCK // ps)
    seq = pl.BlockSpec((1, nh, hd), lambda s, sel_ref, pt_ref: (s, 0, 0))
    hbm = pl.BlockSpec(memory_space=pl.ANY)
    return pl.pallas_call(
        functools.partial(_sample_attn_kernel, n_pages),
        grid_spec=pltpu.PrefetchScalarGridSpec(
            num_scalar_prefetch=2,
            grid=(b,),
            in_specs=[seq, seq, seq, hbm, hbm],
            out_specs=seq,
            scratch_shapes=[pltpu.VMEM((2, nh, n_pages, ps, hd), F32),
                            pltpu.VMEM((2, nh, n_pages, ps, hd), F32),
                            pltpu.SemaphoreType.DMA((2,))],
        ),
        out_shape=jax.ShapeDtypeStruct((b, nh, hd), F32),
        compiler_params=_params("arbitrary"),
        name="moba_sample",
    )(sel, page_table, q, k_new, v_new, cache_k, cache_v)


def kernel(x_prompt, x_sample, state_ssm_re, state_ssm_im, cache_k, cache_v, page_table,
           norm_ffn1, w_ffn1_gate, w_ffn1_up, w_ffn1_down, norm_mix,
           norm_ffn2, w_ffn2_gate, w_ffn2_up, w_ffn2_down,
           ssm_a_re, ssm_a_im, ssm_log_step, ssm_b_re, ssm_b_im, ssm_c_re, ssm_c_im,
           ssm_d, ssm_w_glu_a, ssm_w_glu_b,
           norm_kv, w_k, w_v, attn_w_q, attn_w_o, norm_final):
    n_seq, seq, d = x_prompt.shape
    n_dec, dec_seq, _ = x_sample.shape
    assert dec_seq == 1 and norm_ffn1.shape[0] == 2 and ssm_a_re.shape[0] == 1
    n_pool, page_size, n_heads, head_dim = cache_k.shape
    assert (page_table.shape[1] * page_size) % MOBA_BLOCK == 0

    bf = lambda w: w.astype(BF16)
    ffn1 = (norm_ffn1, bf(w_ffn1_gate), bf(w_ffn1_up), bf(w_ffn1_down))
    ffn2 = (norm_ffn2, bf(w_ffn2_gate), bf(w_ffn2_up), bf(w_ffn2_down))
    glu_a, glu_b = bf(ssm_w_glu_a[0]), bf(ssm_w_glu_b[0])
    wk, wv, wq, wo = bf(w_k), bf(w_v), bf(attn_w_q[0]), bf(attn_w_o[0])
    opw, bst, cst, lb, m0, b0, c1 = ssm_weights(ssm_a_re[0], ssm_a_im[0], ssm_log_step[0], ssm_b_re[0],
                                                ssm_b_im[0], ssm_c_re[0], ssm_c_im[0])

    n_host = 4
    assert n_dec % n_host == 0
    block_sums = []

    def prompt_ffn(x, w, layer, **post):
        first = len(block_sums) * (n_dec // n_host)
        *out, sums = half_ffn_with_page_sums(x, *w, layer, cache_k, page_table, first, n_dec // n_host, **post)
        block_sums.append(sums)
        return out[0] if len(out) == 1 else out

    xp = x_prompt.reshape(n_seq * seq, d)
    xp, u = prompt_ffn(xp, ffn1, 0, post="x+planes", g_post=norm_mix[0])
    y, p_re, p_im = s5_prompt(u, n_seq, opw, bst, cst, lb)
    xp = prompt_ffn(xp, ffn2, 0, pre=("glu", y, norm_mix[0], ssm_d[0], glu_a, glu_b))
    k_p, v_p, kb_p, vb_p, ksum_p = shared_kv(xp, norm_kv, wk, wv, True)
    xp = prompt_ffn(xp, ffn1, 1)
    q_ext = query_select(xp, norm_mix[1], wq, ksum_p, n_seq)
    attn_p = moba_prompt(q_ext, kb_p.reshape(n_seq, seq, d), vb_p.reshape(n_seq, seq, d))
    y_prompt = prompt_ffn(xp, ffn2, 1, pre=("proj", attn_p, wo), post="normed",
                          g_post=norm_final).reshape(n_seq, seq, d)

    xs = x_sample.reshape(n_dec, d)
    xs = half_ffn(xs, *ffn1, 0)
    y, s_re, s_im = s5_step(rmsnorm(xs, norm_mix[0], F32), state_ssm_re[0], state_ssm_im[0], lb, m0, b0, c1)
    xs = ssm_glu(y, xs, norm_mix[0], ssm_d[0], glu_a, glu_b)
    xs = half_ffn(xs, *ffn2, 0)
    k_s, v_s, _, _ = shared_kv(xs, norm_kv, wk, wv, False)
    xs = half_ffn(xs, *ffn1, 1)
    q_s = norm_proj(xs, norm_mix[1], wq)
    per_head = lambda a: a.reshape(n_dec, n_heads, head_dim)
    sel = sample_select(per_head(q_s), jnp.concatenate(block_sums, axis=0))
    attn_s = moba_sample(per_head(q_s), per_head(k_s), per_head(v_s), cache_k, cache_v, page_table, sel)
    xs = proj_residual(attn_s.reshape(n_dec, d).astype(BF16), wo, xs)
    xs = half_ffn(xs, *ffn2, 1)
    y_sample = rmsnorm(xs, norm_final, F32).reshape(n_dec, 1, d)

    heads = lambda a, b, s: a.reshape(b, s, n_heads, head_dim)
    return (y_prompt, y_sample, p_re[None], p_im[None], s_re[None], s_im[None],
            heads(k_p, n_seq, seq), heads(v_p, n_seq, seq), heads(k_s, n_dec, 1), heads(v_s, n_dec, 1))
```
